```python
import math
import jax, jax.numpy as jnp
from jax import lax
import numpy as np

D_MODEL = 1024
BATCH = 4
SEQ = 4096
DEPTH = 1

ATTN_WIDTH = D_MODEL // 2
CONV_WIDTH = D_MODEL - ATTN_WIDTH
N_ATTN_HEADS = 4
ATTN_HEAD_DIM = ATTN_WIDTH // N_ATTN_HEADS // 2
CONV_GROUPS = 8
CONV_K = 3
D_FF = 2816
FFN_CONV_K = 3
Q_BLOCK = 128
LN_EPS = 1e-5
RMS_EPS = 1e-5
DEEPNORM_ALPHA = (2.0 * DEPTH) ** 0.25
DEEPNORM_BETA = (8.0 * DEPTH) ** -0.25
QK_COLS = N_ATTN_HEADS * 2 * ATTN_HEAD_DIM
IN_COLS = 2 * QK_COLS + ATTN_WIDTH + 3 * CONV_WIDTH

kernel_name = "hymba_diffattn_shortconv_convglu_deepnorm"


def lambda_init_fn(layer_idx):
    return 0.8 - 0.6 * math.exp(-0.3 * layer_idx)


def layer_norm(x, g, b):
    xf = x.astype(jnp.float32)
    mu = jnp.mean(xf, axis=-1, keepdims=True)
    var = jnp.mean(jnp.square(xf - mu), axis=-1, keepdims=True)
    y = (xf - mu) * lax.rsqrt(var + LN_EPS) * g.astype(jnp.float32) + b.astype(jnp.float32)
    return y.astype(x.dtype)


def causal_dwconv(u, w):
    k, c = w.shape
    return lax.conv_general_dilated(
        u, w[:, None, :].astype(u.dtype), window_strides=(1,), padding=[(k - 1, 0)],
        dimension_numbers=("NWC", "WIO", "NWC"), feature_group_count=c)


def diff_attention(q, k, v, lam):
    bsz, s, h, _, dh = q.shape
    nb = s // Q_BLOCK
    qb = q.reshape(bsz, nb, Q_BLOCK, h, 2, dh).transpose(1, 0, 3, 4, 2, 5)
    kt = k.transpose(0, 2, 3, 1, 4)
    vt = v.transpose(0, 2, 1, 3)
    kpos = jnp.arange(s)
    scale = dh ** -0.5

    def block(args):
        q_blk, i = args
        sc = jnp.einsum('bhmqd,bhmkd->bhmqk', q_blk, kt).astype(jnp.float32) * scale
        qpos = i * Q_BLOCK + jnp.arange(Q_BLOCK)
        mask = kpos[None, :] <= qpos[:, None]
        sc = jnp.where(mask, sc, -jnp.inf)
        p = jax.nn.softmax(sc, axis=-1)
        a = p[:, :, 0] - lam * p[:, :, 1]
        return jnp.einsum('bhqk,bhkd->bhqd', a.astype(vt.dtype), vt)

    o = lax.map(block, (qb, jnp.arange(nb)))
    return o.transpose(1, 0, 3, 2, 4).reshape(bsz, s, h, 2 * dh)


def hybrid_mixer(x, w_in, lq1, lk1, lq2, lk2, attn_norm_g, conv_w, w_out, layer_idx):
    bsz, s, _ = x.shape
    proj = jnp.einsum('bsd,dc->bsc', x, w_in)
    o1 = QK_COLS
    o2 = o1 + QK_COLS
    o3 = o2 + ATTN_WIDTH
    o4 = o3 + CONV_WIDTH
    o5 = o4 + CONV_WIDTH
    q = proj[..., :o1].reshape(bsz, s, N_ATTN_HEADS, 2, ATTN_HEAD_DIM)
    k = proj[..., o1:o2].reshape(bsz, s, N_ATTN_HEADS, 2, ATTN_HEAD_DIM)
    v = proj[..., o2:o3].reshape(bsz, s, N_ATTN_HEADS, 2 * ATTN_HEAD_DIM)
    gate_b = proj[..., o3:o4]
    gate_c = proj[..., o4:o5]
    hc = proj[..., o5:]

    lam_init = lambda_init_fn(layer_idx)
    lam = (jnp.exp(jnp.sum(lq1.astype(jnp.float32) * lk1.astype(jnp.float32)))
           - jnp.exp(jnp.sum(lq2.astype(jnp.float32) * lk2.astype(jnp.float32))) + lam_init)
    o = diff_attention(q, k, v, lam).astype(jnp.float32)
    o = o * lax.rsqrt(jnp.mean(jnp.square(o), axis=-1, keepdims=True) + RMS_EPS)
    o = o * attn_norm_g.astype(jnp.float32) * (1.0 - lam_init)
    attn_out = o.reshape(bsz, s, ATTN_WIDTH).astype(x.dtype)

    conv_out = gate_b * causal_dwconv(gate_c * hc, conv_w)

    cat = jnp.concatenate([attn_out, conv_out], axis=-1)
    return jnp.einsum('bsc,cd->bsd', cat, w_out)


def conv_glu(x, w_up, conv_w, conv_b, w_down):
    up = jnp.einsum('bsd,df->bsf', x, w_up)
    g, val = up[..., :D_FF], up[..., D_FF:]
    g = causal_dwconv(g, conv_w) + conv_b
    return jnp.einsum('bsf,fd->bsd', jax.nn.silu(g) * val, w_down)


def setup_inputs(seed: int = 0) -> dict:
    key = jax.random.key(seed)
    ks = jax.random.split(key, 20)
    L = DEPTH
    x = jax.random.normal(ks[0], (BATCH, SEQ, D_MODEL), jnp.float32)
    col_scale = jnp.concatenate([
        jnp.ones((2 * QK_COLS,), jnp.float32),
        jnp.full((ATTN_WIDTH,), DEEPNORM_BETA, jnp.float32),
        jnp.ones((2 * CONV_WIDTH,), jnp.float32),
        jnp.full((CONV_WIDTH,), DEEPNORM_BETA, jnp.float32)])
    w_in = jax.random.normal(ks[1], (L, D_MODEL, IN_COLS), jnp.float32) * D_MODEL ** -0.5 * col_scale
    lambda_q1 = 0.1 * jax.random.normal(ks[2], (L, ATTN_HEAD_DIM), jnp.float32)
    lambda_k1 = 0.1 * jax.random.normal(ks[3], (L, ATTN_HEAD_DIM), jnp.float32)
    lambda_q2 = 0.1 * jax.random.normal(ks[4], (L, ATTN_HEAD_DIM), jnp.float32)
    lambda_k2 = 0.1 * jax.random.normal(ks[5], (L, ATTN_HEAD_DIM), jnp.float32)
    attn_norm_g = 1.0 + 0.02 * jax.random.normal(ks[6], (L, 2 * ATTN_HEAD_DIM), jnp.float32)
    conv_w = jax.random.normal(ks[7], (L, CONV_K, CONV_WIDTH), jnp.float32) * CONV_K ** -0.5
    w_out = jax.random.normal(ks[8], (L, D_MODEL, D_MODEL), jnp.float32) * D_MODEL ** -0.5 * DEEPNORM_BETA
    ln1_g = 1.0 + 0.02 * jax.random.normal(ks[9], (L, D_MODEL), jnp.float32)
    ln1_b = 0.02 * jax.random.normal(ks[10], (L, D_MODEL), jnp.float32)
    ffn_w_up = jax.random.normal(ks[11], (L, D_MODEL, 2 * D_FF), jnp.float32) * D_MODEL ** -0.5 * DEEPNORM_BETA
    ffn_conv_w = jax.random.normal(ks[12], (L, FFN_CONV_K, D_FF), jnp.float32) * FFN_CONV_K ** -0.5
    ffn_conv_b = 0.02 * jax.random.normal(ks[13], (L, D_FF), jnp.float32)
    ffn_w_down = jax.random.normal(ks[14], (L, D_FF, D_MODEL), jnp.float32) * D_FF ** -0.5 * DEEPNORM_BETA
    ln2_g = 1.0 + 0.02 * jax.random.normal(ks[15], (L, D_MODEL), jnp.float32)
    ln2_b = 0.02 * jax.random.normal(ks[16], (L, D_MODEL), jnp.float32)
    return {"x": x, "w_in": w_in, "lambda_q1": lambda_q1, "lambda_k1": lambda_k1,
            "lambda_q2": lambda_q2, "lambda_k2": lambda_k2, "attn_norm_g": attn_norm_g,
            "conv_w": conv_w, "w_out": w_out, "ln1_g": ln1_g, "ln1_b": ln1_b,
            "ffn_w_up": ffn_w_up, "ffn_conv_w": ffn_conv_w, "ffn_conv_b": ffn_conv_b,
            "ffn_w_down": ffn_w_down, "ln2_g": ln2_g, "ln2_b": ln2_b}


def reference(x, w_in, lambda_q1, lambda_k1, lambda_q2, lambda_k2, attn_norm_g, conv_w, w_out,
              ln1_g, ln1_b, ffn_w_up, ffn_conv_w, ffn_conv_b, ffn_w_down, ln2_g, ln2_b):
    h = x
    for l in range(DEPTH):
        mix = hybrid_mixer(h, w_in[l], lambda_q1[l], lambda_k1[l], lambda_q2[l], lambda_k2[l],
                           attn_norm_g[l], conv_w[l], w_out[l], l)
        h = layer_norm(DEEPNORM_ALPHA * h + mix, ln1_g[l], ln1_b[l])
        f = conv_glu(h, ffn_w_up[l], ffn_conv_w[l], ffn_conv_b[l], ffn_w_down[l])
        h = layer_norm(DEEPNORM_ALPHA * h + f, ln2_g[l], ln2_b[l])
    return h
```

```python
import functools
import math

import jax
import jax.numpy as jnp
from jax import lax
from jax.experimental import pallas as pl
from jax.experimental.pallas import tpu as pltpu

F32 = jnp.float32
BF16 = jnp.bfloat16

N_HEADS = 4
QK_DIM = 64
V_DIM = 2 * QK_DIM
ATTN_W = N_HEADS * V_DIM
CONV_K = 3
LN_EPS = 1e-5
RMS_EPS = 1e-5
LAMBDA_INIT = 0.8 - 0.6 * math.exp(-0.3 * 0)
DEEPNORM_ALPHA = 2.0 ** 0.25

SUBLANES = 8
MIB = 1024 * 1024


def _resident(shape):
    return pl.BlockSpec(shape, lambda *_: (0,) * len(shape), pipeline_mode=pl.Buffered(1))


def _causal_conv3(buf_ref, cur, w, rows):
    buf_ref[pl.ds(SUBLANES, rows), :] = cur
    prev1 = buf_ref[pl.ds(SUBLANES - 1, rows), :]
    prev2 = buf_ref[pl.ds(SUBLANES - 2, rows), :]
    out = w[0:1, :] * prev2 + w[1:2, :] * prev1 + w[2:3, :] * cur
    buf_ref[pl.ds(0, SUBLANES), :] = buf_ref[pl.ds(rows, SUBLANES), :]
    return out


def _inproj_kernel(x_ref, wkg_ref, wqvt_ref, convw_ref, k_ref, qt_ref, vt_ref, conv_ref, ubuf_ref, *, rows):
    @pl.when(pl.program_id(1) == 0)
    def _():
        ubuf_ref[pl.ds(0, SUBLANES), :] = jnp.zeros((SUBLANES, ATTN_W), F32)

    xb = x_ref[...].astype(BF16)
    kg = jnp.dot(xb, wkg_ref[...], preferred_element_type=F32)
    k_ref[...] = kg[:, :ATTN_W].astype(BF16)
    gate_b = kg[:, ATTN_W:2 * ATTN_W]
    u = kg[:, 2 * ATTN_W:3 * ATTN_W] * kg[:, 3 * ATTN_W:]
    conv_ref[...] = (gate_b * _causal_conv3(ubuf_ref, u, convw_ref[...], rows)).astype(BF16)

    qvt = lax.dot_general(wqvt_ref[...], xb, (((1,), (1,)), ((), ())), preferred_element_type=F32)
    qt_ref[...] = (qvt[:ATTN_W] * (QK_DIM ** -0.5)).astype(BF16)
    vt_ref[...] = qvt[ATTN_W:].astype(BF16)


def _inproj(x, wkg, wqvt, conv_w, *, rows):
    b, s, d = x.shape
    grid = (b, s // rows)
    row_blk = pl.BlockSpec((None, rows, ATTN_W), lambda bi, ti: (bi, ti, 0))
    col_blk = pl.BlockSpec((None, ATTN_W, rows), lambda bi, ti: (bi, 0, ti))
    return pl.pallas_call(
        functools.partial(_inproj_kernel, rows=rows),
        grid=grid,
        in_specs=[
            pl.BlockSpec((None, rows, d), lambda bi, ti: (bi, ti, 0)),
            _resident(wkg.shape),
            _resident(wqvt.shape),
            _resident(conv_w.shape),
        ],
        out_specs=[row_blk, col_blk, col_blk, row_blk],
        out_shape=[
            jax.ShapeDtypeStruct((b, s, ATTN_W), BF16),
            jax.ShapeDtypeStruct((b, ATTN_W, s), BF16),
            jax.ShapeDtypeStruct((b, ATTN_W, s), BF16),
            jax.ShapeDtypeStruct((b, s, ATTN_W), BF16),
        ],
        scratch_shapes=[pltpu.VMEM((rows + SUBLANES, ATTN_W), F32)],
        compiler_params=pltpu.CompilerParams(
            dimension_semantics=("arbitrary", "arbitrary"), vmem_limit_bytes=40 * MIB),
        name="inproj",
    )(x, wkg, wqvt, conv_w)


def _attn_kernel(qt_ref, k_ref, vt_ref, lq1_ref, lk1_ref, lq2_ref, lk2_ref, g_ref, o_ref, acc_ref, *, tq):
    qi = pl.program_id(2)
    qt = qt_ref[...]
    top = lax.broadcasted_iota(jnp.int32, qt.shape, 0) < QK_DIM
    zero = jnp.zeros_like(qt)
    qpad = (jnp.where(top, qt, zero), jnp.where(top, zero, qt))

    acc_ref[...] = jnp.zeros(acc_ref.shape, F32)

    def tile(j, carry, masked):
        start = pl.multiple_of(j * tq, tq)
        kblk = k_ref[pl.ds(start, tq), :]
        vtblk = vt_ref[:, pl.ds(start, tq)]
        out = []
        for m in range(2):
            m_old, l_old = carry[2 * m], carry[2 * m + 1]
            s = jnp.dot(kblk, qpad[m], preferred_element_type=F32)
            if masked:
                kpos = lax.broadcasted_iota(jnp.int32, s.shape, 0)
                qpos = lax.broadcasted_iota(jnp.int32, s.shape, 1)
                s = jnp.where(kpos <= qpos, s, -jnp.inf)
            m_new = jnp.maximum(m_old, jnp.max(s, axis=0, keepdims=True))
            alpha = jnp.exp(m_old - m_new)
            p = jnp.exp(s - m_new)
            l_new = alpha * l_old + jnp.sum(p, axis=0, keepdims=True)
            pv = jnp.dot(vtblk, p.astype(BF16), preferred_element_type=F32)
            acc_ref[m] = alpha * acc_ref[m] + pv
            out += [m_new, l_new]
        return tuple(out)

    neg = jnp.full((1, tq), -jnp.inf, F32)
    zer = jnp.zeros((1, tq), F32)
    carry = lax.fori_loop(0, qi, lambda j, c: tile(j, c, False), (neg, zer, neg, zer))
    _, l1, _, l2 = tile(qi, carry, True)

    lam = (jnp.exp(jnp.sum(lq1_ref[...] * lk1_ref[...], keepdims=True))
           - jnp.exp(jnp.sum(lq2_ref[...] * lk2_ref[...], keepdims=True)) + LAMBDA_INIT)
    o = acc_ref[0] / l1 - lam * (acc_ref[1] / l2)
    o = o * lax.rsqrt(jnp.mean(o * o, axis=0, keepdims=True) + RMS_EPS)
    o = o * g_ref[...] * (1.0 - LAMBDA_INIT)
    o_ref[...] = o.T.astype(BF16)


def _attention(qt, k, vt, lq1, lk1, lq2, lk2, g_col, *, tq):
    b, s, _ = k.shape
    lam_blk = pl.BlockSpec((1, QK_DIM), lambda bi, hi, qi: (0, 0))
    return pl.pallas_call(
        functools.partial(_attn_kernel, tq=tq),
        grid=(b, N_HEADS, s // tq),
        in_specs=[
            pl.BlockSpec((None, V_DIM, tq), lambda bi, hi, qi: (bi, hi, qi)),
            pl.BlockSpec((None, s, V_DIM), lambda bi, hi, qi: (bi, 0, hi)),
            pl.BlockSpec((None, V_DIM, s), lambda bi, hi, qi: (bi, hi, 0)),
            lam_blk, lam_blk, lam_blk, lam_blk,
            pl.BlockSpec((V_DIM, 1), lambda bi, hi, qi: (0, 0)),
        ],
        out_specs=pl.BlockSpec((None, tq, V_DIM), lambda bi, hi, qi: (bi, qi, hi)),
        out_shape=jax.ShapeDtypeStruct((b, s, ATTN_W), BF16),
        scratch_shapes=[pltpu.VMEM((2, V_DIM, tq), F32)],
        compiler_params=pltpu.CompilerParams(
            dimension_semantics=("arbitrary", "arbitrary", "arbitrary"), vmem_limit_bytes=32 * MIB),
        name="diffattn",
    )(qt, k, vt, lq1, lk1, lq2, lk2, g_col)


def _layer_norm(v, g, b):
    mu = jnp.mean(v, axis=-1, keepdims=True)
    c = v - mu
    var = jnp.mean(c * c, axis=-1, keepdims=True)
    return c * lax.rsqrt(var + LN_EPS) * g + b


def _tail_kernel(x_ref, attn_ref, conv_ref, woa_ref, woc_ref, ln1g_ref, ln1b_ref, wup_ref, fcw_ref, fcb_ref,
                 wdn_ref, ln2g_ref, ln2b_ref, o_ref, gbuf_ref, gcarry_ref, *, rows, d_ff, chunks):
    @pl.when(pl.program_id(1) == 0)
    def _():
        gcarry_ref[...] = jnp.zeros(gcarry_ref.shape, F32)

    mix = (jnp.dot(attn_ref[...], woa_ref[...], preferred_element_type=F32)
           + jnp.dot(conv_ref[...], woc_ref[...], preferred_element_type=F32))
    h1 = _layer_norm(DEEPNORM_ALPHA * x_ref[...] + mix, ln1g_ref[...], ln1b_ref[...])
    h1b = h1.astype(BF16)

    ffn = None
    for c0, cw in chunks:
        g = jnp.dot(h1b, wup_ref[:, c0:c0 + cw], preferred_element_type=F32)
        val = jnp.dot(h1b, wup_ref[:, d_ff + c0:d_ff + c0 + cw], preferred_element_type=F32)
        buf = gbuf_ref.at[:, 0:cw]
        buf[pl.ds(0, SUBLANES), :] = gcarry_ref[:, c0:c0 + cw]
        gc = _causal_conv3(buf, g, fcw_ref[:, c0:c0 + cw], rows) + fcb_ref[:, c0:c0 + cw]
        gcarry_ref[:, c0:c0 + cw] = buf[pl.ds(0, SUBLANES), :]
        act = (gc * jax.nn.sigmoid(gc) * val).astype(BF16)
        part = jnp.dot(act, wdn_ref[c0:c0 + cw, :], preferred_element_type=F32)
        ffn = part if ffn is None else ffn + part
    o_ref[...] = _layer_norm(DEEPNORM_ALPHA * h1 + ffn, ln2g_ref[...], ln2b_ref[...])


def _tail(x, attn, conv, woa, woc, ln1g, ln1b, wup, fcw, fcb, wdn, ln2g, ln2b, *, rows, chunk):
    b, s, d = x.shape
    d_ff = wdn.shape[0]
    chunks = tuple((c0, min(chunk, d_ff - c0)) for c0 in range(0, d_ff, chunk))
    half_blk = pl.BlockSpec((None, rows, ATTN_W), lambda bi, ti: (bi, ti, 0))
    full_blk = pl.BlockSpec((None, rows, d), lambda bi, ti: (bi, ti, 0))
    return pl.pallas_call(
        functools.partial(_tail_kernel, rows=rows, d_ff=d_ff, chunks=chunks),
        grid=(b, s // rows),
        in_specs=[full_blk, half_blk, half_blk,
                  _resident(woa.shape), _resident(woc.shape), _resident(ln1g.shape), _resident(ln1b.shape),
                  _resident(wup.shape), _resident(fcw.shape), _resident(fcb.shape), _resident(wdn.shape),
                  _resident(ln2g.shape), _resident(ln2b.shape)],
        out_specs=full_blk,
        out_shape=jax.ShapeDtypeStruct((b, s, d), x.dtype),
        scratch_shapes=[pltpu.VMEM((rows + SUBLANES, chunk), F32), pltpu.VMEM((SUBLANES, d_ff), F32)],
        compiler_params=pltpu.CompilerParams(
            dimension_semantics=("arbitrary", "arbitrary"), vmem_limit_bytes=56 * MIB),
        name="tail",
    )(x, attn, conv, woa, woc, ln1g, ln1b, wup, fcw, fcb, wdn, ln2g, ln2b)


def kernel(x, w_in, lambda_q1, lambda_k1, lambda_q2, lambda_k2, attn_norm_g, conv_w, w_out, ln1_g, ln1_b,
           ffn_w_up, ffn_conv_w, ffn_conv_b, ffn_w_down, ln2_g, ln2_b):
    w = w_in[0]
    wkg = jnp.concatenate([w[:, ATTN_W:2 * ATTN_W], w[:, 3 * ATTN_W:]], axis=1).astype(BF16)
    wqvt = jnp.concatenate([w[:, :ATTN_W], w[:, 2 * ATTN_W:3 * ATTN_W]], axis=1).T.astype(BF16)
    wo = w_out[0].astype(BF16)

    k, qt, vt, conv = _inproj(x, wkg, wqvt, conv_w[0], rows=512)
    attn = _attention(qt, k, vt, lambda_q1, lambda_k1, lambda_q2, lambda_k2,
                      attn_norm_g[0].reshape(V_DIM, 1), tq=256)
    return _tail(x, attn, conv, wo[:ATTN_W], wo[ATTN_W:], ln1_g, ln1_b, ffn_w_up[0].astype(BF16),
                 ffn_conv_w[0], ffn_conv_b, ffn_w_down[0].astype(BF16), ln2_g, ln2_b, rows=512, chunk=512)
```

```python
import functools
import math

import jax
import jax.numpy as jnp
from jax import lax
from jax.experimental import pallas as pl
from jax.experimental.pallas import tpu as pltpu

F32 = jnp.float32
BF16 = jnp.bfloat16

N_HEADS = 4
QK_DIM = 64
V_DIM = 2 * QK_DIM
ATTN_W = N_HEADS * V_DIM
CONV_K = 3
LN_EPS = 1e-5
RMS_EPS = 1e-5
LAMBDA_INIT = 0.8 - 0.6 * math.exp(-0.3 * 0)
DEEPNORM_ALPHA = 2.0 ** 0.25
SCORE_SCALE = QK_DIM ** -0.5 * math.log2(math.e)

L_ROWS = 16
SUBLANES = 8
MIB = 1024 * 1024


def _resident(shape):
    return pl.BlockSpec(shape, lambda *_: (0,) * len(shape), pipeline_mode=pl.Buffered(1))


def _causal_conv3(buf_ref, cur, w, rows):
    buf_ref[pl.ds(SUBLANES, rows), :] = cur
    prev1 = buf_ref[pl.ds(SUBLANES - 1, rows), :]
    prev2 = buf_ref[pl.ds(SUBLANES - 2, rows), :]
    out = w[0:1, :] * prev2 + w[1:2, :] * prev1 + w[2:3, :] * cur
    buf_ref[pl.ds(0, SUBLANES), :] = buf_ref[pl.ds(rows, SUBLANES), :]
    return out


def _inproj_kernel(x_ref, win_ref, wqvt_ref, convw_ref, k_ref, qt_ref, vt_ref, conv_ref, ubuf_ref, *, rows):
    @pl.when(pl.program_id(1) == 0)
    def _():
        ubuf_ref[pl.ds(0, SUBLANES), :] = jnp.zeros((SUBLANES, ATTN_W), F32)

    xb = x_ref[...].astype(BF16)
    gates = jnp.dot(xb, win_ref[:, 3 * ATTN_W:], preferred_element_type=F32)
    k_ref[...] = jnp.dot(xb, win_ref[:, ATTN_W:2 * ATTN_W], preferred_element_type=F32).astype(BF16)
    qvt = lax.dot_general(wqvt_ref[...], xb, (((1,), (1,)), ((), ())), preferred_element_type=F32)
    qt_ref[...] = (qvt[:ATTN_W] * SCORE_SCALE).astype(BF16)
    vt_ref[...] = qvt[ATTN_W:].astype(BF16)

    u = gates[:, ATTN_W:2 * ATTN_W] * gates[:, 2 * ATTN_W:]
    conv_ref[...] = (gates[:, :ATTN_W] * _causal_conv3(ubuf_ref, u, convw_ref[...], rows)).astype(BF16)


def _inproj(x, win, wqvt, conv_w, *, rows):
    b, s, d = x.shape
    grid = (b, s // rows)
    row_blk = pl.BlockSpec((None, rows, ATTN_W), lambda bi, ti: (bi, ti, 0))
    col_blk = pl.BlockSpec((None, ATTN_W, rows), lambda bi, ti: (bi, 0, ti))
    return pl.pallas_call(
        functools.partial(_inproj_kernel, rows=rows),
        grid=grid,
        in_specs=[
            pl.BlockSpec((None, rows, d), lambda bi, ti: (bi, ti, 0)),
            _resident(win.shape),
            _resident(wqvt.shape),
            _resident(conv_w.shape),
        ],
        out_specs=[row_blk, col_blk, col_blk, row_blk],
        out_shape=[
            jax.ShapeDtypeStruct((b, s, ATTN_W), BF16),
            jax.ShapeDtypeStruct((b, ATTN_W, s), BF16),
            jax.ShapeDtypeStruct((b, ATTN_W, s), BF16),
            jax.ShapeDtypeStruct((b, s, ATTN_W), BF16),
        ],
        scratch_shapes=[pltpu.VMEM((rows + SUBLANES, ATTN_W), F32)],
        compiler_params=pltpu.CompilerParams(
            dimension_semantics=("arbitrary", "arbitrary"), vmem_limit_bytes=40 * MIB),
        name="inproj",
    )(x, win, wqvt, conv_w)


def _attn_kernel(qt_ref, k_ref, vt_ref, lq1_ref, lk1_ref, lq2_ref, lk2_ref, g_ref, o_ref,
                 qpad_ref, m_ref, acc_ref, s_ref, bias_ref, *, tq, tk):
    qi = pl.program_id(1)
    top = lax.broadcasted_iota(jnp.int32, (V_DIM, tq), 0) < QK_DIM
    for h in range(N_HEADS):
        qt = qt_ref[h * V_DIM:(h + 1) * V_DIM, :]
        zero = jnp.zeros_like(qt)
        qpad_ref[2 * h] = jnp.where(top, qt, zero)
        qpad_ref[2 * h + 1] = jnp.where(top, zero, qt)
    m_ref[...] = jnp.full(m_ref.shape, -jnp.inf, F32)
    acc_ref[...] = jnp.zeros(acc_ref.shape, F32)
    ones_rows = jnp.ones((L_ROWS, tk), BF16)

    def tile(j, masked):
        start = pl.multiple_of(j * tk, tk)
        if masked:
            kpos = start + lax.broadcasted_iota(jnp.int32, (tk, tq), 0)
            qpos = qi * tq + lax.broadcasted_iota(jnp.int32, (tk, tq), 1)
            bias_ref[...] = jnp.where(kpos <= qpos, 0.0, -jnp.inf).astype(F32)

        def scores(c):
            h = c // 2
            kblk = k_ref[pl.ds(start, tk), h * V_DIM:(h + 1) * V_DIM]
            s = jnp.dot(kblk, qpad_ref[c], preferred_element_type=F32)
            s_ref[c % 2] = s + bias_ref[...] if masked else s

        scores(0)
        for c in range(2 * N_HEADS):
            h = c // 2
            if c + 1 < 2 * N_HEADS:
                scores(c + 1)
            m_old = m_ref[c:c + 1, :]
            m_new = jnp.maximum(m_old, jnp.max(s_ref[c % 2], axis=0, keepdims=True))
            alpha = jnp.exp2(m_old - m_new)
            p = jnp.exp2(s_ref[c % 2] - m_new).astype(BF16)
            m_ref[c:c + 1, :] = m_new
            vtblk = vt_ref[h * V_DIM:(h + 1) * V_DIM, pl.ds(start, tk)]
            pv = jnp.dot(jnp.concatenate([vtblk, ones_rows], axis=0), p, preferred_element_type=F32)
            acc_ref[c] = alpha * acc_ref[c] + pv

    def full_tile(j, carry):
        tile(j, False)
        return carry

    ratio = tq // tk
    lax.fori_loop(0, qi * ratio, full_tile, 0)
    for d in range(ratio):
        tile(qi * ratio + d, True)

    lam = (jnp.exp(jnp.sum(lq1_ref[...] * lk1_ref[...], keepdims=True))
           - jnp.exp(jnp.sum(lq2_ref[...] * lk2_ref[...], keepdims=True)) + LAMBDA_INIT)
    for h in range(N_HEADS):
        c1, c2 = 2 * h, 2 * h + 1
        o = (acc_ref[c1, :V_DIM, :] / acc_ref[c1, V_DIM:V_DIM + 1, :]
             - lam * (acc_ref[c2, :V_DIM, :] / acc_ref[c2, V_DIM:V_DIM + 1, :]))
        o = o * lax.rsqrt(jnp.mean(o * o, axis=0, keepdims=True) + RMS_EPS)
        o = o * g_ref[...] * (1.0 - LAMBDA_INIT)
        o_ref[:, h * V_DIM:(h + 1) * V_DIM] = o.T.astype(BF16)


def _attention(qt, k, vt, lq1, lk1, lq2, lk2, g_col, *, tq, tk):
    b, s, _ = k.shape
    n_chains = 2 * N_HEADS
    lam_blk = pl.BlockSpec((1, QK_DIM), lambda bi, qi: (0, 0))
    return pl.pallas_call(
        functools.partial(_attn_kernel, tq=tq, tk=tk),
        grid=(b, s // tq),
        in_specs=[
            pl.BlockSpec((None, ATTN_W, tq), lambda bi, qi: (bi, 0, qi)),
            pl.BlockSpec((None, s, ATTN_W), lambda bi, qi: (bi, 0, 0)),
            pl.BlockSpec((None, ATTN_W, s), lambda bi, qi: (bi, 0, 0)),
            lam_blk, lam_blk, lam_blk, lam_blk,
            pl.BlockSpec((V_DIM, 1), lambda bi, qi: (0, 0)),
        ],
        out_specs=pl.BlockSpec((None, tq, ATTN_W), lambda bi, qi: (bi, qi, 0)),
        out_shape=jax.ShapeDtypeStruct((b, s, ATTN_W), BF16),
        scratch_shapes=[
            pltpu.VMEM((n_chains, V_DIM, tq), BF16),
            pltpu.VMEM((n_chains, tq), F32),
            pltpu.VMEM((n_chains, V_DIM + L_ROWS, tq), F32),
            pltpu.VMEM((2, tk, tq), F32),
            pltpu.VMEM((tk, tq), F32),
        ],
        compiler_params=pltpu.CompilerParams(
            dimension_semantics=("arbitrary", "arbitrary"), vmem_limit_bytes=48 * MIB),
        name="diffattn",
    )(qt, k, vt, lq1, lk1, lq2, lk2, g_col)


def _layer_norm(v, g, b):
    mu = jnp.mean(v, axis=-1, keepdims=True)
    c = v - mu
    var = jnp.mean(c * c, axis=-1, keepdims=True)
    return c * lax.rsqrt(var + LN_EPS) * g + b


def _tail_kernel(x_ref, attn_ref, conv_ref, wo_ref, ln1g_ref, ln1b_ref, wup_ref, fcw_ref, fcb_ref,
                 wdn_ref, ln2g_ref, ln2b_ref, o_ref, gbuf_ref, gcarry_ref, *, rows, d_ff, chunks):
    @pl.when(pl.program_id(1) == 0)
    def _():
        gcarry_ref[...] = jnp.zeros(gcarry_ref.shape, F32)

    mix = (jnp.dot(attn_ref[...], wo_ref[:ATTN_W, :], preferred_element_type=F32)
           + jnp.dot(conv_ref[...], wo_ref[ATTN_W:, :], preferred_element_type=F32))
    h1 = _layer_norm(DEEPNORM_ALPHA * x_ref[...] + mix, ln1g_ref[...], ln1b_ref[...])
    h1b = h1.astype(BF16)

    ffn = None
    for c0, cw in chunks:
        g = jnp.dot(h1b, wup_ref[:, c0:c0 + cw], preferred_element_type=F32)
        val = jnp.dot(h1b, wup_ref[:, d_ff + c0:d_ff + c0 + cw], preferred_element_type=F32)
        buf = gbuf_ref.at[:, 0:cw]
        buf[pl.ds(0, SUBLANES), :] = gcarry_ref[:, c0:c0 + cw]
        gc = _causal_conv3(buf, g, fcw_ref[:, c0:c0 + cw], rows) + fcb_ref[:, c0:c0 + cw]
        gcarry_ref[:, c0:c0 + cw] = buf[pl.ds(0, SUBLANES), :]
        act = (gc * jax.nn.sigmoid(gc) * val).astype(BF16)
        part = jnp.dot(act, wdn_ref[c0:c0 + cw, :], preferred_element_type=F32)
        ffn = part if ffn is None else ffn + part
    o_ref[...] = _layer_norm(DEEPNORM_ALPHA * h1 + ffn, ln2g_ref[...], ln2b_ref[...])


def _tail(x, attn, conv, wo, ln1g, ln1b, wup, fcw, fcb, wdn, ln2g, ln2b, *, rows, chunk):
    b, s, d = x.shape
    d_ff = wdn.shape[0]
    chunks = tuple((c0, min(chunk, d_ff - c0)) for c0 in range(0, d_ff, chunk))
    half_blk = pl.BlockSpec((None, rows, ATTN_W), lambda bi, ti: (bi, ti, 0))
    full_blk = pl.BlockSpec((None, rows, d), lambda bi, ti: (bi, ti, 0))
    return pl.pallas_call(
        functools.partial(_tail_kernel, rows=rows, d_ff=d_ff, chunks=chunks),
        grid=(b, s // rows),
        in_specs=[full_blk, half_blk, half_blk,
                  _resident(wo.shape), _resident(ln1g.shape), _resident(ln1b.shape),
                  _resident(wup.shape), _resident(fcw.shape), _resident(fcb.shape), _resident(wdn.shape),
                  _resident(ln2g.shape), _resident(ln2b.shape)],
        out_specs=full_blk,
        out_shape=jax.ShapeDtypeStruct((b, s, d), x.dtype),
        scratch_shapes=[pltpu.VMEM((rows + SUBLANES, chunk), F32),
                        pltpu.VMEM((SUBLANES, d_ff), F32)],
        compiler_params=pltpu.CompilerParams(
            dimension_semantics=("arbitrary", "arbitrary"), vmem_limit_bytes=56 * MIB),
        name="tail",
    )(x, attn, conv, wo, ln1g, ln1b, wup, fcw, fcb, wdn, ln2g, ln2b)


def kernel(x, w_in, lambda_q1, lambda_k1, lambda_q2, lambda_k2, attn_norm_g, conv_w, w_out, ln1_g, ln1_b,
           ffn_w_up, ffn_conv_w, ffn_conv_b, ffn_w_down, ln2_g, ln2_b):
    win = w_in[0].astype(BF16)
    wqvt = jnp.concatenate([win[:, :ATTN_W], win[:, 2 * ATTN_W:3 * ATTN_W]], axis=1).T

    k, qt, vt, conv = _inproj(x, win, wqvt, conv_w[0], rows=512)
    attn = _attention(qt, k, vt, lambda_q1, lambda_k1, lambda_q2, lambda_k2,
                      attn_norm_g[0].reshape(V_DIM, 1), tq=512, tk=512)
    return _tail(x, attn, conv, w_out[0].astype(BF16), ln1_g, ln1_b, ffn_w_up[0].astype(BF16),
                 ffn_conv_w[0], ffn_conv_b, ffn_w_down[0].astype(BF16), ln2_g, ln2_b, rows=512, chunk=512)
```

```python
import functools
import math

import jax
import jax.numpy as jnp
from jax import lax
from jax.experimental import pallas as pl
from jax.experimental.pallas import tpu as pltpu

F32 = jnp.float32
BF16 = jnp.bfloat16

N_HEADS = 4
QK_DIM = 64
V_DIM = 2 * QK_DIM
ATTN_W = N_HEADS * V_DIM
CONV_K = 3
LN_EPS = 1e-5
RMS_EPS = 1e-5
LAMBDA_INIT = 0.8 - 0.6 * math.exp(-0.3 * 0)
DEEPNORM_ALPHA = 2.0 ** 0.25
SCORE_SCALE = QK_DIM ** -0.5 * math.log2(math.e)

L_ROWS = 16
SUBLANES = 8
MIB = 1024 * 1024


def _resident(shape):
    return pl.BlockSpec(shape, lambda *_: (0,) * len(shape), pipeline_mode=pl.Buffered(1))


def _causal_conv3(buf_ref, cur, w, rows):
    buf_ref[pl.ds(SUBLANES, rows), :] = cur
    prev1 = buf_ref[pl.ds(SUBLANES - 1, rows), :]
    prev2 = buf_ref[pl.ds(SUBLANES - 2, rows), :]
    out = w[0:1, :] * prev2 + w[1:2, :] * prev1 + w[2:3, :] * cur
    buf_ref[pl.ds(0, SUBLANES), :] = buf_ref[pl.ds(rows, SUBLANES), :]
    return out


def _inproj_kernel(x_ref, win_ref, convw_ref, k_ref, qt_ref, vt_ref, conv_ref, ubuf_ref, gates_ref,
                   *, rows):
    @pl.when(pl.program_id(1) == 0)
    def _():
        ubuf_ref[pl.ds(0, SUBLANES), :] = jnp.zeros((SUBLANES, ATTN_W), F32)

    xb = x_ref[...].astype(BF16)
    gates_ref[...] = jnp.dot(xb, win_ref[:, 3 * ATTN_W:], preferred_element_type=F32)
    k_ref[...] = jnp.dot(xb, win_ref[:, ATTN_W:2 * ATTN_W], preferred_element_type=F32).astype(BF16)
    q = jnp.dot(xb, win_ref[:, :ATTN_W], preferred_element_type=F32) * SCORE_SCALE
    qt_ref[...] = q.astype(BF16).T
    v = jnp.dot(xb, win_ref[:, 2 * ATTN_W:3 * ATTN_W], preferred_element_type=F32)
    vt_ref[...] = v.astype(BF16).T

    u = gates_ref[:, ATTN_W:2 * ATTN_W] * gates_ref[:, 2 * ATTN_W:]
    conv_ref[...] = (gates_ref[:, :ATTN_W] * _causal_conv3(ubuf_ref, u, convw_ref[...], rows)).astype(BF16)


def _inproj(x, win, conv_w, *, rows):
    b, s, d = x.shape
    grid = (b, s // rows)
    row_blk = pl.BlockSpec((None, rows, ATTN_W), lambda bi, ti: (bi, ti, 0))
    col_blk = pl.BlockSpec((None, ATTN_W, rows), lambda bi, ti: (bi, 0, ti))
    return pl.pallas_call(
        functools.partial(_inproj_kernel, rows=rows),
        grid=grid,
        in_specs=[
            pl.BlockSpec((None, rows, d), lambda bi, ti: (bi, ti, 0)),
            _resident(win.shape),
            _resident(conv_w.shape),
        ],
        out_specs=[row_blk, col_blk, col_blk, row_blk],
        out_shape=[
            jax.ShapeDtypeStruct((b, s, ATTN_W), BF16),
            jax.ShapeDtypeStruct((b, ATTN_W, s), BF16),
            jax.ShapeDtypeStruct((b, ATTN_W, s), BF16),
            jax.ShapeDtypeStruct((b, s, ATTN_W), BF16),
        ],
        scratch_shapes=[pltpu.VMEM((rows + SUBLANES, ATTN_W), F32),
                        pltpu.VMEM((rows, 3 * ATTN_W), F32)],
        compiler_params=pltpu.CompilerParams(
            dimension_semantics=("arbitrary", "arbitrary"), vmem_limit_bytes=40 * MIB),
        name="inproj",
    )(x, win, conv_w)


def _attn_kernel(qt_ref, k_ref, vt_ref, lq1_ref, lk1_ref, lq2_ref, lk2_ref, g_ref, o_ref,
                 qpad_ref, m_ref, acc_ref, s_ref, bias_ref, *, t):
    qi = pl.program_id(1)
    n_chains = 2 * N_HEADS
    top = lax.broadcasted_iota(jnp.int32, (V_DIM, t), 0) < QK_DIM
    for h in range(N_HEADS):
        qt = qt_ref[h * V_DIM:(h + 1) * V_DIM, :]
        zero = jnp.zeros_like(qt)
        qpad_ref[2 * h] = jnp.where(top, qt, zero)
        qpad_ref[2 * h + 1] = jnp.where(top, zero, qt)
    m_ref[...] = jnp.full(m_ref.shape, -jnp.inf, F32)
    acc_ref[...] = jnp.zeros(acc_ref.shape, F32)
    ones_rows = jnp.ones((L_ROWS, t), BF16)

    def scores(start, c, masked):
        h = c // 2
        kblk = k_ref[pl.ds(start, t), h * V_DIM:(h + 1) * V_DIM]
        s = jnp.dot(kblk, qpad_ref[c], preferred_element_type=F32)
        s_ref[c % 2] = s + bias_ref[...] if masked else s

    def tile(j, masked):
        start = pl.multiple_of(j * t, t)
        for c in range(n_chains):
            h = c // 2
            if c + 1 < n_chains:
                scores(start, c + 1, masked)
            elif not masked:
                scores(pl.multiple_of(start + t, t), 0, False)
            m_old = m_ref[c:c + 1, :]
            m_new = jnp.maximum(m_old, jnp.max(s_ref[c % 2], axis=0, keepdims=True))
            alpha = jnp.exp2(m_old - m_new)
            p = jnp.exp2(s_ref[c % 2] - m_new).astype(BF16)
            m_ref[c:c + 1, :] = m_new
            vtblk = vt_ref[h * V_DIM:(h + 1) * V_DIM, pl.ds(start, t)]
            pv = jnp.dot(jnp.concatenate([vtblk, ones_rows], axis=0), p, preferred_element_type=F32)
            acc_ref[c] = alpha * acc_ref[c] + pv

    def full_tile(j, carry):
        tile(j, False)
        return carry

    scores(0, 0, False)
    lax.fori_loop(0, qi, full_tile, 0)
    kpos = lax.broadcasted_iota(jnp.int32, (t, t), 0)
    qpos = lax.broadcasted_iota(jnp.int32, (t, t), 1)
    bias_ref[...] = jnp.where(kpos <= qpos, 0.0, -jnp.inf).astype(F32)
    s_ref[0] = s_ref[0] + bias_ref[...]
    tile(qi, True)

    lam = (jnp.exp(jnp.sum(lq1_ref[...] * lk1_ref[...], keepdims=True))
           - jnp.exp(jnp.sum(lq2_ref[...] * lk2_ref[...], keepdims=True)) + LAMBDA_INIT)
    for h in range(N_HEADS):
        c1, c2 = 2 * h, 2 * h + 1
        o = (acc_ref[c1, :V_DIM, :] / acc_ref[c1, V_DIM:V_DIM + 1, :]
             - lam * (acc_ref[c2, :V_DIM, :] / acc_ref[c2, V_DIM:V_DIM + 1, :]))
        o = o * lax.rsqrt(jnp.mean(o * o, axis=0, keepdims=True) + RMS_EPS)
        o = o * g_ref[...] * (1.0 - LAMBDA_INIT)
        o_ref[:, h * V_DIM:(h + 1) * V_DIM] = o.T.astype(BF16)


def _attention(qt, k, vt, lq1, lk1, lq2, lk2, g_col, *, t):
    b, s, _ = k.shape
    n_chains = 2 * N_HEADS
    lam_blk = pl.BlockSpec((1, QK_DIM), lambda bi, qi: (0, 0))
    return pl.pallas_call(
        functools.partial(_attn_kernel, t=t),
        grid=(b, s // t),
        in_specs=[
            pl.BlockSpec((None, ATTN_W, t), lambda bi, qi: (bi, 0, qi)),
            pl.BlockSpec((None, s, ATTN_W), lambda bi, qi: (bi, 0, 0)),
            pl.BlockSpec((None, ATTN_W, s), lambda bi, qi: (bi, 0, 0)),
            lam_blk, lam_blk, lam_blk, lam_blk,
            pl.BlockSpec((V_DIM, 1), lambda bi, qi: (0, 0)),
        ],
        out_specs=pl.BlockSpec((None, t, ATTN_W), lambda bi, qi: (bi, qi, 0)),
        out_shape=jax.ShapeDtypeStruct((b, s, ATTN_W), BF16),
        scratch_shapes=[
            pltpu.VMEM((n_chains, V_DIM, t), BF16),
            pltpu.VMEM((n_chains, t), F32),
            pltpu.VMEM((n_chains, V_DIM + L_ROWS, t), F32),
            pltpu.VMEM((2, t, t), F32),
            pltpu.VMEM((t, t), F32),
        ],
        compiler_params=pltpu.CompilerParams(
            dimension_semantics=("arbitrary", "arbitrary"), vmem_limit_bytes=48 * MIB),
        name="diffattn",
    )(qt, k, vt, lq1, lk1, lq2, lk2, g_col)


def _layer_norm(v, g, b):
    mu = jnp.mean(v, axis=-1, keepdims=True)
    c = v - mu
    var = jnp.mean(c * c, axis=-1, keepdims=True)
    return c * lax.rsqrt(var + LN_EPS) * g + b


def _tail_kernel(x_ref, attn_ref, conv_ref, wo_ref, ln1g_ref, ln1b_ref, wup_ref, fcw_ref, fcb_ref,
                 wdn_ref, ln2g_ref, ln2b_ref, o_ref, gbuf_ref, val_ref, gcarry_ref, h1_ref, h1b_ref, act_ref,
                 *, rows, d_ff, chunks):
    @pl.when(pl.program_id(1) == 0)
    def _():
        gcarry_ref[...] = jnp.zeros(gcarry_ref.shape, F32)

    mix = (jnp.dot(attn_ref[...], wo_ref[:ATTN_W, :], preferred_element_type=F32)
           + jnp.dot(conv_ref[...], wo_ref[ATTN_W:, :], preferred_element_type=F32))
    h1 = _layer_norm(DEEPNORM_ALPHA * x_ref[...] + mix, ln1g_ref[...], ln1b_ref[...])
    h1_ref[...] = h1
    h1b_ref[...] = h1.astype(BF16)

    for n, (c0, cw) in enumerate(chunks):
        cols = slice(c0, c0 + cw)
        buf = gbuf_ref.at[n % 2, :, 0:cw]
        val = val_ref.at[n % 2, :, 0:cw]
        buf[pl.ds(0, SUBLANES), :] = gcarry_ref[:, cols]
        buf[pl.ds(SUBLANES, rows), :] = jnp.dot(h1b_ref[...], wup_ref[:, cols], preferred_element_type=F32)
        val[...] = jnp.dot(h1b_ref[...], wup_ref[:, d_ff + c0:d_ff + c0 + cw], preferred_element_type=F32)
        gcarry_ref[:, cols] = buf[pl.ds(rows, SUBLANES), :]
        w = fcw_ref[:, cols]
        gc = (w[0:1, :] * buf[pl.ds(SUBLANES - 2, rows), :] + w[1:2, :] * buf[pl.ds(SUBLANES - 1, rows), :]
              + w[2:3, :] * buf[pl.ds(SUBLANES, rows), :] + fcb_ref[:, cols])
        act_ref[:, cols] = (gc * jax.nn.sigmoid(gc) * val[...]).astype(BF16)
    ffn = jnp.dot(act_ref[...], wdn_ref[...], preferred_element_type=F32)
    o_ref[...] = _layer_norm(DEEPNORM_ALPHA * h1_ref[...] + ffn, ln2g_ref[...], ln2b_ref[...])


def _tail(x, attn, conv, wo, ln1g, ln1b, wup, fcw, fcb, wdn, ln2g, ln2b, *, rows, chunk):
    b, s, d = x.shape
    d_ff = wdn.shape[0]
    chunks = tuple((c0, min(chunk, d_ff - c0)) for c0 in range(0, d_ff, chunk))
    half_blk = pl.BlockSpec((None, rows, ATTN_W), lambda bi, ti: (bi, ti, 0))
    full_blk = pl.BlockSpec((None, rows, d), lambda bi, ti: (bi, ti, 0))
    return pl.pallas_call(
        functools.partial(_tail_kernel, rows=rows, d_ff=d_ff, chunks=chunks),
        grid=(b, s // rows),
        in_specs=[full_blk, half_blk, half_blk,
                  _resident(wo.shape), _resident(ln1g.shape), _resident(ln1b.shape),
                  _resident(wup.shape), _resident(fcw.shape), _resident(fcb.shape), _resident(wdn.shape),
                  _resident(ln2g.shape), _resident(ln2b.shape)],
        out_specs=full_blk,
        out_shape=jax.ShapeDtypeStruct((b, s, d), x.dtype),
        scratch_shapes=[pltpu.VMEM((2, rows + SUBLANES, chunk), F32),
                        pltpu.VMEM((2, rows, chunk), F32),
                        pltpu.VMEM((SUBLANES, d_ff), F32),
                        pltpu.VMEM((rows, d), F32),
                        pltpu.VMEM((rows, d), BF16),
                        pltpu.VMEM((rows, d_ff), BF16)],
        compiler_params=pltpu.CompilerParams(
            dimension_semantics=("arbitrary", "arbitrary"), vmem_limit_bytes=56 * MIB),
        name="tail",
    )(x, attn, conv, wo, ln1g, ln1b, wup, fcw, fcb, wdn, ln2g, ln2b)


def kernel(x, w_in, lambda_q1, lambda_k1, lambda_q2, lambda_k2, attn_norm_g, conv_w, w_out, ln1_g, ln1_b,
           ffn_w_up, ffn_conv_w, ffn_conv_b, ffn_w_down, ln2_g, ln2_b):
    win = w_in[0].astype(BF16)

    k, qt, vt, conv = _inproj(x, win, conv_w[0], rows=512)
    attn = _attention(qt, k, vt, lambda_q1, lambda_k1, lambda_q2, lambda_k2,
                      attn_norm_g[0].reshape(V_DIM, 1), t=512)
    return _tail(x, attn, conv, w_out[0].astype(BF16), ln1_g, ln1_b, ffn_w_up[0].astype(BF16),
                 ffn_conv_w[0], ffn_conv_b, ffn_w_down[0].astype(BF16), ln2_g, ln2_b, rows=512, chunk=512)
```

```python
import functools
import math

import jax
import jax.numpy as jnp
from jax import lax
from jax.experimental import pallas as pl
from jax.experimental.pallas import tpu as pltpu

F32 = jnp.float32
BF16 = jnp.bfloat16

N_HEADS = 4
QK_DIM = 64
V_DIM = 2 * QK_DIM
ATTN_W = N_HEADS * V_DIM
CONV_K = 3
LN_EPS = 1e-5
RMS_EPS = 1e-5
LAMBDA_INIT = 0.8 - 0.6 * math.exp(-0.3 * 0)
DEEPNORM_ALPHA = 2.0 ** 0.25
SCORE_SCALE = QK_DIM ** -0.5 * math.log2(math.e)

L_ROWS = 16
SUBLANES = 8
MIB = 1024 * 1024


def _resident(shape):
    return pl.BlockSpec(shape, lambda *_: (0,) * len(shape), pipeline_mode=pl.Buffered(1))


def _causal_conv3(cur, hist, w):
    row = lax.broadcasted_iota(jnp.int32, hist.shape, 0)

    def shifted(s):
        rolled = pltpu.roll(cur, s, axis=0)
        head = jnp.where(row < s, pltpu.roll(hist, s, axis=0), rolled[:SUBLANES])
        return jnp.concatenate([head, rolled[SUBLANES:]], axis=0)

    return w[0:1, :] * shifted(2) + w[1:2, :] * shifted(1) + w[2:3, :] * cur


def _inproj_kernel(x_ref, win_ref, convw_ref, k_ref, qt_ref, vt_ref, conv_ref, uhist_ref, gates_ref,
                   *, rows):
    @pl.when(pl.program_id(1) == 0)
    def _():
        uhist_ref[...] = jnp.zeros(uhist_ref.shape, F32)

    xb = x_ref[...].astype(BF16)
    q = jnp.dot(xb, win_ref[:, :ATTN_W], preferred_element_type=F32) * SCORE_SCALE
    qt_ref[...] = q.astype(BF16).T
    v = jnp.dot(xb, win_ref[:, 2 * ATTN_W:3 * ATTN_W], preferred_element_type=F32)
    vt_ref[...] = v.astype(BF16).T
    k_ref[...] = jnp.dot(xb, win_ref[:, ATTN_W:2 * ATTN_W], preferred_element_type=F32).astype(BF16)

    half = ATTN_W // 2
    for c0 in (0, half):
        for part in range(3):
            src = 3 * ATTN_W + part * ATTN_W + c0
            gates_ref[:, part * ATTN_W + c0:part * ATTN_W + c0 + half] = jnp.dot(
                xb, win_ref[:, src:src + half], preferred_element_type=F32)
    for c0 in (0, half):
        cols = slice(c0, c0 + half)
        u = gates_ref[:, ATTN_W + c0:ATTN_W + c0 + half] * gates_ref[:, 2 * ATTN_W + c0:2 * ATTN_W + c0 + half]
        conv = _causal_conv3(u, uhist_ref[:, cols], convw_ref[:, cols])
        conv_ref[:, cols] = (gates_ref[:, cols] * conv).astype(BF16)
        uhist_ref[:, cols] = u[rows - SUBLANES:]


def _inproj(x, win, conv_w, *, rows):
    b, s, d = x.shape
    grid = (b, s // rows)
    row_blk = pl.BlockSpec((None, rows, ATTN_W), lambda bi, ti: (bi, ti, 0))
    col_blk = pl.BlockSpec((None, ATTN_W, rows), lambda bi, ti: (bi, 0, ti))
    return pl.pallas_call(
        functools.partial(_inproj_kernel, rows=rows),
        grid=grid,
        in_specs=[
            pl.BlockSpec((None, rows, d), lambda bi, ti: (bi, ti, 0)),
            _resident(win.shape),
            _resident(conv_w.shape),
        ],
        out_specs=[row_blk, col_blk, col_blk, row_blk],
        out_shape=[
            jax.ShapeDtypeStruct((b, s, ATTN_W), BF16),
            jax.ShapeDtypeStruct((b, ATTN_W, s), BF16),
            jax.ShapeDtypeStruct((b, ATTN_W, s), BF16),
            jax.ShapeDtypeStruct((b, s, ATTN_W), BF16),
        ],
        scratch_shapes=[pltpu.VMEM((SUBLANES, ATTN_W), F32),
                        pltpu.VMEM((rows, 3 * ATTN_W), F32)],
        compiler_params=pltpu.CompilerParams(
            dimension_semantics=("arbitrary", "arbitrary"), vmem_limit_bytes=40 * MIB),
        name="inproj",
    )(x, win, conv_w)


def _attn_kernel(qt_ref, k_ref, vt_ref, lq1_ref, lk1_ref, lq2_ref, lk2_ref, g_ref, o_ref,
                 qpad_ref, m_ref, acc_ref, s_ref, bias_ref, *, t):
    qi = pl.program_id(1)
    n_chains = 2 * N_HEADS
    hq = t // 2
    left, right = slice(0, hq), slice(hq, t)
    top_keys, bot_keys = slice(0, hq), slice(hq, t)
    top = lax.broadcasted_iota(jnp.int32, (V_DIM, t), 0) < QK_DIM
    for h in range(N_HEADS):
        qt = qt_ref[h * V_DIM:(h + 1) * V_DIM, :]
        zero = jnp.zeros_like(qt)
        qpad_ref[2 * h] = jnp.where(top, qt, zero)
        qpad_ref[2 * h + 1] = jnp.where(top, zero, qt)
    m_ref[...] = jnp.full(m_ref.shape, -jnp.inf, F32)
    acc_ref[...] = jnp.zeros(acc_ref.shape, F32)

    @pl.when(jnp.logical_and(pl.program_id(0) == 0, qi == 0))
    def _():
        kpos = lax.broadcasted_iota(jnp.int32, (hq, hq), 0)
        qpos = lax.broadcasted_iota(jnp.int32, (hq, hq), 1)
        bias_ref[...] = jnp.where(kpos <= qpos, 0.0, -jnp.inf).astype(F32)

    def softmax_pv(c, s_cols, keys, vt_cols, m_new):
        h = c // 2
        p = jnp.exp2(s_ref[c % 2, keys, s_cols] - m_new).astype(BF16)
        vtblk = vt_ref[h * V_DIM:(h + 1) * V_DIM, vt_cols]
        ones_rows = jnp.ones((L_ROWS, vtblk.shape[1]), BF16)
        return jnp.dot(jnp.concatenate([vtblk, ones_rows], axis=0), p, preferred_element_type=F32)

    def scores(start, c):
        h = c // 2
        kblk = k_ref[pl.ds(start, t), h * V_DIM:(h + 1) * V_DIM]
        s_ref[c % 2] = jnp.dot(kblk, qpad_ref[c], preferred_element_type=F32)

    def full_tile(j, carry):
        start = pl.multiple_of(j * t, t)
        for c in range(n_chains):
            if c + 1 < n_chains:
                scores(start, c + 1)
            else:
                scores(pl.multiple_of(start + t, t), 0)
            m_old = m_ref[c:c + 1, :]
            m_new = jnp.maximum(m_old, jnp.max(s_ref[c % 2], axis=0, keepdims=True))
            m_ref[c:c + 1, :] = m_new
            pv = softmax_pv(c, slice(None), slice(None), pl.ds(start, t), m_new)
            acc_ref[c] = jnp.exp2(m_old - m_new) * acc_ref[c] + pv
        return carry

    def diag_scores(start, c):
        h = c // 2
        ktop = k_ref[pl.ds(start, hq), h * V_DIM:(h + 1) * V_DIM]
        kbot = k_ref[pl.ds(pl.multiple_of(start + hq, hq), hq), h * V_DIM:(h + 1) * V_DIM]
        s_ref[c % 2, top_keys, :] = jnp.dot(ktop, qpad_ref[c], preferred_element_type=F32)
        s_ref[c % 2, bot_keys, right] = jnp.dot(kbot, qpad_ref[c, :, right], preferred_element_type=F32)

    def diag_tile(start):
        for c in range(n_chains):
            if c + 1 < n_chains:
                diag_scores(start, c + 1)
            slot = c % 2
            s_ref[slot, top_keys, left] = s_ref[slot, top_keys, left] + bias_ref[...]
            s_ref[slot, bot_keys, right] = s_ref[slot, bot_keys, right] + bias_ref[...]
            m_old = m_ref[c:c + 1, :]
            m_top = jnp.max(s_ref[slot, top_keys, :], axis=0, keepdims=True)
            m_bot = jnp.max(s_ref[slot, bot_keys, right], axis=0, keepdims=True)
            m_new = jnp.maximum(m_old, jnp.concatenate(
                [m_top[:, left], jnp.maximum(m_top[:, right], m_bot)], axis=1))
            m_ref[c:c + 1, :] = m_new
            pv = softmax_pv(c, slice(None), top_keys, pl.ds(start, hq), m_new)
            pv_r = softmax_pv(c, right, bot_keys, pl.ds(pl.multiple_of(start + hq, hq), hq), m_new[:, right])
            new = jnp.exp2(m_old - m_new) * acc_ref[c] + pv
            acc_ref[c, :, left] = new[:, left]
            acc_ref[c, :, right] = new[:, right] + pv_r

    scores(0, 0)
    lax.fori_loop(0, qi, full_tile, 0)
    diag_tile(pl.multiple_of(qi * t, t))

    lam = (jnp.exp(jnp.sum(lq1_ref[...] * lk1_ref[...], keepdims=True))
           - jnp.exp(jnp.sum(lq2_ref[...] * lk2_ref[...], keepdims=True)) + LAMBDA_INIT)
    for h in range(N_HEADS):
        c1, c2 = 2 * h, 2 * h + 1
        o = (acc_ref[c1, :V_DIM, :] / acc_ref[c1, V_DIM:V_DIM + 1, :]
             - lam * (acc_ref[c2, :V_DIM, :] / acc_ref[c2, V_DIM:V_DIM + 1, :]))
        o = o * lax.rsqrt(jnp.mean(o * o, axis=0, keepdims=True) + RMS_EPS)
        o = o * g_ref[...] * (1.0 - LAMBDA_INIT)
        o_ref[:, h * V_DIM:(h + 1) * V_DIM] = o.T.astype(BF16)


def _attention(qt, k, vt, lq1, lk1, lq2, lk2, g_col, *, t):
    b, s, _ = k.shape
    n_chains = 2 * N_HEADS
    lam_blk = pl.BlockSpec((1, QK_DIM), lambda bi, qi: (0, 0))
    return pl.pallas_call(
        functools.partial(_attn_kernel, t=t),
        grid=(b, s // t),
        in_specs=[
            pl.BlockSpec((None, ATTN_W, t), lambda bi, qi: (bi, 0, qi)),
            pl.BlockSpec((None, s, ATTN_W), lambda bi, qi: (bi, 0, 0)),
            pl.BlockSpec((None, ATTN_W, s), lambda bi, qi: (bi, 0, 0)),
            lam_blk, lam_blk, lam_blk, lam_blk,
            pl.BlockSpec((V_DIM, 1), lambda bi, qi: (0, 0)),
        ],
        out_specs=pl.BlockSpec((None, t, ATTN_W), lambda bi, qi: (bi, qi, 0)),
        out_shape=jax.ShapeDtypeStruct((b, s, ATTN_W), BF16),
        scratch_shapes=[
            pltpu.VMEM((n_chains, V_DIM, t), BF16),
            pltpu.VMEM((n_chains, t), F32),
            pltpu.VMEM((n_chains, V_DIM + L_ROWS, t), F32),
            pltpu.VMEM((2, t, t), F32),
            pltpu.VMEM((t // 2, t // 2), F32),
        ],
        compiler_params=pltpu.CompilerParams(
            dimension_semantics=("arbitrary", "arbitrary"), vmem_limit_bytes=48 * MIB),
        name="diffattn",
    )(qt, k, vt, lq1, lk1, lq2, lk2, g_col)


def _layer_norm(v, g, b):
    mu = jnp.mean(v, axis=-1, keepdims=True)
    c = v - mu
    var = jnp.mean(c * c, axis=-1, keepdims=True)
    return c * lax.rsqrt(var + LN_EPS) * g + b


def _tail_kernel(x_ref, attn_ref, conv_ref, wo_ref, ln1g_ref, ln1b_ref, wup_ref, fcw_ref, fcb_ref,
                 wdn_ref, ln2g_ref, ln2b_ref, o_ref, gate_ref, val_ref, ghist_ref, h1_ref, h1b_ref, act_ref,
                 *, rows, d_ff, chunks):
    @pl.when(pl.program_id(1) == 0)
    def _():
        ghist_ref[...] = jnp.zeros(ghist_ref.shape, F32)

    halves = (pl.ds(0, rows // 2), pl.ds(rows // 2, rows // 2))
    for rs in halves:
        mix = (jnp.dot(attn_ref[rs, :], wo_ref[:ATTN_W, :], preferred_element_type=F32)
               + jnp.dot(conv_ref[rs, :], wo_ref[ATTN_W:, :], preferred_element_type=F32))
        h1 = _layer_norm(DEEPNORM_ALPHA * x_ref[rs, :] + mix, ln1g_ref[...], ln1b_ref[...])
        h1_ref[rs, :] = h1
        h1b_ref[rs, :] = h1.astype(BF16)

    for n, (c0, cw) in enumerate(chunks):
        cols = slice(c0, c0 + cw)
        gate = gate_ref.at[n % 2, :, 0:cw]
        val = val_ref.at[n % 2, :, 0:cw]
        gate[...] = jnp.dot(h1b_ref[...], wup_ref[:, cols], preferred_element_type=F32)
        val[...] = jnp.dot(h1b_ref[...], wup_ref[:, d_ff + c0:d_ff + c0 + cw], preferred_element_type=F32)
        gc = _causal_conv3(gate[...], ghist_ref[:, cols], fcw_ref[:, cols]) + fcb_ref[:, cols]
        ghist_ref[:, cols] = gate[pl.ds(rows - SUBLANES, SUBLANES), :]
        act_ref[:, cols] = (gc * jax.nn.sigmoid(gc) * val[...]).astype(BF16)
    for rs in halves:
        ffn = jnp.dot(act_ref[rs, :], wdn_ref[...], preferred_element_type=F32)
        o_ref[rs, :] = _layer_norm(DEEPNORM_ALPHA * h1_ref[rs, :] + ffn, ln2g_ref[...], ln2b_ref[...])


def _tail(x, attn, conv, wo, ln1g, ln1b, wup, fcw, fcb, wdn, ln2g, ln2b, *, rows, chunk):
    b, s, d = x.shape
    d_ff = wdn.shape[0]
    chunks = tuple((c0, min(chunk, d_ff - c0)) for c0 in range(0, d_ff, chunk))
    half_blk = pl.BlockSpec((None, rows, ATTN_W), lambda bi, ti: (bi, ti, 0))
    full_blk = pl.BlockSpec((None, rows, d), lambda bi, ti: (bi, ti, 0))
    return pl.pallas_call(
        functools.partial(_tail_kernel, rows=rows, d_ff=d_ff, chunks=chunks),
        grid=(b, s // rows),
        in_specs=[full_blk, half_blk, half_blk,
                  _resident(wo.shape), _resident(ln1g.shape), _resident(ln1b.shape),
                  _resident(wup.shape), _resident(fcw.shape), _resident(fcb.shape), _resident(wdn.shape),
                  _resident(ln2g.shape), _resident(ln2b.shape)],
        out_specs=full_blk,
        out_shape=jax.ShapeDtypeStruct((b, s, d), x.dtype),
        scratch_shapes=[pltpu.VMEM((2, rows, chunk), F32),
                        pltpu.VMEM((2, rows, chunk), F32),
                        pltpu.VMEM((SUBLANES, d_ff), F32),
                        pltpu.VMEM((rows, d), F32),
                        pltpu.VMEM((rows, d), BF16),
                        pltpu.VMEM((rows, d_ff), BF16)],
        compiler_params=pltpu.CompilerParams(
            dimension_semantics=("arbitrary", "arbitrary"), vmem_limit_bytes=56 * MIB),
        name="tail",
    )(x, attn, conv, wo, ln1g, ln1b, wup, fcw, fcb, wdn, ln2g, ln2b)


def kernel(x, w_in, lambda_q1, lambda_k1, lambda_q2, lambda_k2, attn_norm_g, conv_w, w_out, ln1_g, ln1_b,
           ffn_w_up, ffn_conv_w, ffn_conv_b, ffn_w_down, ln2_g, ln2_b):
    win = w_in[0].astype(BF16)

    k, qt, vt, conv = _inproj(x, win, conv_w[0], rows=512)
    attn = _attention(qt, k, vt, lambda_q1, lambda_k1, lambda_q2, lambda_k2,
                      attn_norm_g[0].reshape(V_DIM, 1), t=512)
    return _tail(x, attn, conv, w_out[0].astype(BF16), ln1_g, ln1_b, ffn_w_up[0].astype(BF16),
                 ffn_conv_w[0], ffn_conv_b, ffn_w_down[0].astype(BF16), ln2_g, ln2_b, rows=512, chunk=512)
```

```python
import functools
import math

import jax
import jax.numpy as jnp
from jax import lax
from jax.experimental import pallas as pl
from jax.experimental.pallas import tpu as pltpu

F32 = jnp.float32
BF16 = jnp.bfloat16

N_HEADS = 4
QK_DIM = 64
V_DIM = 2 * QK_DIM
ATTN_W = N_HEADS * V_DIM
CONV_K = 3
LN_EPS = 1e-5
RMS_EPS = 1e-5
LAMBDA_INIT = 0.8 - 0.6 * math.exp(-0.3 * 0)
DEEPNORM_ALPHA = 2.0 ** 0.25
SCORE_SCALE = QK_DIM ** -0.5 * math.log2(math.e)

L_ROWS = 16
SUB_ROWS = 512
SUBLANES = 8
MIB = 1024 * 1024


def _resident(shape):
    return pl.BlockSpec(shape, lambda *_: (0,) * len(shape), pipeline_mode=pl.Buffered(1))


def _causal_conv3(cur, hist, w):
    row = lax.broadcasted_iota(jnp.int32, hist.shape, 0)

    def shifted(s):
        rolled = pltpu.roll(cur, s, axis=0)
        head = jnp.where(row < s, pltpu.roll(hist, s, axis=0), rolled[:SUBLANES])
        return jnp.concatenate([head, rolled[SUBLANES:]], axis=0)

    return w[0:1, :] * shifted(2) + w[1:2, :] * shifted(1) + w[2:3, :] * cur


def _inproj_kernel(x_ref, win_ref, convw_ref, k_ref, qt_ref, vt_ref, conv_ref, uhist_ref, gates_ref,
                   *, rows):
    @pl.when(pl.program_id(1) == 0)
    def _():
        uhist_ref[...] = jnp.zeros(uhist_ref.shape, F32)

    xb = x_ref[...].astype(BF16)
    q = jnp.dot(xb, win_ref[:, :ATTN_W], preferred_element_type=F32) * SCORE_SCALE
    qt_ref[...] = q.astype(BF16).T
    v = jnp.dot(xb, win_ref[:, 2 * ATTN_W:3 * ATTN_W], preferred_element_type=F32)
    vt_ref[...] = v.astype(BF16).T
    k_ref[...] = jnp.dot(xb, win_ref[:, ATTN_W:2 * ATTN_W], preferred_element_type=F32).astype(BF16)

    half = ATTN_W // 2
    for c0 in (0, half):
        for part in range(3):
            src = 3 * ATTN_W + part * ATTN_W + c0
            gates_ref[:, part * ATTN_W + c0:part * ATTN_W + c0 + half] = jnp.dot(
                xb, win_ref[:, src:src + half], preferred_element_type=F32)
    for c0 in (0, half):
        cols = slice(c0, c0 + half)
        u = gates_ref[:, ATTN_W + c0:ATTN_W + c0 + half] * gates_ref[:, 2 * ATTN_W + c0:2 * ATTN_W + c0 + half]
        conv = _causal_conv3(u, uhist_ref[:, cols], convw_ref[:, cols])
        conv_ref[:, cols] = (gates_ref[:, cols] * conv).astype(BF16)
        uhist_ref[:, cols] = u[rows - SUBLANES:]


def _inproj(x, win, conv_w, *, rows):
    b, s, d = x.shape
    grid = (b, s // rows)
    row_blk = pl.BlockSpec((None, rows, ATTN_W), lambda bi, ti: (bi, ti, 0))
    col_blk = pl.BlockSpec((None, ATTN_W, rows), lambda bi, ti: (bi, 0, ti))
    return pl.pallas_call(
        functools.partial(_inproj_kernel, rows=rows),
        grid=grid,
        in_specs=[
            pl.BlockSpec((None, rows, d), lambda bi, ti: (bi, ti, 0)),
            _resident(win.shape),
            _resident(conv_w.shape),
        ],
        out_specs=[row_blk, col_blk, col_blk, row_blk],
        out_shape=[
            jax.ShapeDtypeStruct((b, s, ATTN_W), BF16),
            jax.ShapeDtypeStruct((b, ATTN_W, s), BF16),
            jax.ShapeDtypeStruct((b, ATTN_W, s), BF16),
            jax.ShapeDtypeStruct((b, s, ATTN_W), BF16),
        ],
        scratch_shapes=[pltpu.VMEM((SUBLANES, ATTN_W), F32),
                        pltpu.VMEM((rows, 3 * ATTN_W), F32)],
        compiler_params=pltpu.CompilerParams(
            dimension_semantics=("arbitrary", "arbitrary"), vmem_limit_bytes=40 * MIB),
        name="inproj",
    )(x, win, conv_w)


def _attn_kernel(qt_ref, k_ref, vt_ref, lq1_ref, lk1_ref, lq2_ref, lk2_ref, g_ref, o_ref,
                 qpad_ref, m_ref, acc_ref, s_ref, bias_ref, *, t, q_per_step):
    n_chains = 2 * N_HEADS
    hq = t // 2
    left, right = slice(0, hq), slice(hq, t)
    top_keys, bot_keys = slice(0, hq), slice(hq, t)

    @pl.when(jnp.logical_and(pl.program_id(0) == 0, pl.program_id(1) == 0))
    def _():
        kpos = lax.broadcasted_iota(jnp.int32, (hq, hq), 0)
        qpos = lax.broadcasted_iota(jnp.int32, (hq, hq), 1)
        bias_ref[...] = jnp.where(kpos <= qpos, 0.0, -jnp.inf).astype(F32)

    lam = (jnp.exp(jnp.sum(lq1_ref[...] * lk1_ref[...], keepdims=True))
           - jnp.exp(jnp.sum(lq2_ref[...] * lk2_ref[...], keepdims=True)) + LAMBDA_INIT)

    def query_tile(u):
        qi = pl.program_id(1) * q_per_step + u
        qcols = slice(u * t, (u + 1) * t)
        qpad, m, acc = qpad_ref.at[u], m_ref.at[u], acc_ref.at[u]
        top = lax.broadcasted_iota(jnp.int32, (V_DIM, t), 0) < QK_DIM
        for h in range(N_HEADS):
            qt = qt_ref[h * V_DIM:(h + 1) * V_DIM, qcols]
            zero = jnp.zeros_like(qt)
            qpad[2 * h] = jnp.where(top, qt, zero)
            qpad[2 * h + 1] = jnp.where(top, zero, qt)

        def softmax_pv(c, s_cols, keys, vt_cols, m_new):
            h = c // 2
            p = jnp.exp2(s_ref[c % 2, keys, s_cols] - m_new).astype(BF16)
            vtblk = vt_ref[h * V_DIM:(h + 1) * V_DIM, vt_cols]
            ones_rows = jnp.ones((L_ROWS, vtblk.shape[1]), BF16)
            return jnp.dot(jnp.concatenate([vtblk, ones_rows], axis=0), p, preferred_element_type=F32)

        def scores(start, c):
            h = c // 2
            kblk = k_ref[pl.ds(start, t), h * V_DIM:(h + 1) * V_DIM]
            s_ref[c % 2] = jnp.dot(kblk, qpad[c], preferred_element_type=F32)

        def diag_scores(start, c):
            h = c // 2
            ktop = k_ref[pl.ds(start, hq), h * V_DIM:(h + 1) * V_DIM]
            kbot = k_ref[pl.ds(pl.multiple_of(start + hq, hq), hq), h * V_DIM:(h + 1) * V_DIM]
            s_ref[c % 2, top_keys, :] = jnp.dot(ktop, qpad[c], preferred_element_type=F32)
            s_ref[c % 2, bot_keys, right] = jnp.dot(kbot, qpad[c, :, right], preferred_element_type=F32)

        def diag_tile(start):
            for c in range(n_chains):
                if c + 1 < n_chains:
                    diag_scores(start, c + 1)
                else:
                    scores(0, 0)
                slot = c % 2
                s_ref[slot, top_keys, left] = s_ref[slot, top_keys, left] + bias_ref[...]
                s_ref[slot, bot_keys, right] = s_ref[slot, bot_keys, right] + bias_ref[...]
                m_top = jnp.max(s_ref[slot, top_keys, :], axis=0, keepdims=True)
                m_bot = jnp.max(s_ref[slot, bot_keys, right], axis=0, keepdims=True)
                m_new = jnp.concatenate([m_top[:, left], jnp.maximum(m_top[:, right], m_bot)], axis=1)
                m[c:c + 1, :] = m_new
                pv = softmax_pv(c, slice(None), top_keys, pl.ds(start, hq), m_new)
                pv_r = softmax_pv(c, right, bot_keys, pl.ds(pl.multiple_of(start + hq, hq), hq), m_new[:, right])
                acc[c, :, left] = pv[:, left]
                acc[c, :, right] = pv[:, right] + pv_r

        def full_tile(j, carry):
            start = pl.multiple_of(j * t, t)
            for c in range(n_chains):
                if c + 1 < n_chains:
                    scores(start, c + 1)
                else:
                    scores(pl.multiple_of(start + t, t), 0)
                m_old = m[c:c + 1, :]
                m_new = jnp.maximum(m_old, jnp.max(s_ref[c % 2], axis=0, keepdims=True))
                m[c:c + 1, :] = m_new
                pv = softmax_pv(c, slice(None), slice(None), pl.ds(start, t), m_new)
                acc[c] = jnp.exp2(m_old - m_new) * acc[c] + pv
            return carry

        diag_start = pl.multiple_of(qi * t, t)
        diag_scores(diag_start, 0)
        diag_tile(diag_start)
        lax.fori_loop(0, qi, full_tile, 0)

        for h in range(N_HEADS):
            c1, c2 = 2 * h, 2 * h + 1
            o = (acc[c1, :V_DIM, :] / acc[c1, V_DIM:V_DIM + 1, :]
                 - lam * (acc[c2, :V_DIM, :] / acc[c2, V_DIM:V_DIM + 1, :]))
            o = o * lax.rsqrt(jnp.mean(o * o, axis=0, keepdims=True) + RMS_EPS)
            o = o * g_ref[...] * (1.0 - LAMBDA_INIT)
            o_ref[qcols, h * V_DIM:(h + 1) * V_DIM] = o.T.astype(BF16)

    for u in range(q_per_step):
        query_tile(u)


def _attention(qt, k, vt, lq1, lk1, lq2, lk2, g_col, *, t, q_per_step):
    b, s, _ = k.shape
    n_chains = 2 * N_HEADS
    tq = q_per_step * t
    lam_blk = pl.BlockSpec((1, QK_DIM), lambda bi, qi: (0, 0))
    return pl.pallas_call(
        functools.partial(_attn_kernel, t=t, q_per_step=q_per_step),
        grid=(b, s // tq),
        in_specs=[
            pl.BlockSpec((None, ATTN_W, tq), lambda bi, qi: (bi, 0, qi)),
            pl.BlockSpec((None, s, ATTN_W), lambda bi, qi: (bi, 0, 0)),
            pl.BlockSpec((None, ATTN_W, s), lambda bi, qi: (bi, 0, 0)),
            lam_blk, lam_blk, lam_blk, lam_blk,
            pl.BlockSpec((V_DIM, 1), lambda bi, qi: (0, 0)),
        ],
        out_specs=pl.BlockSpec((None, tq, ATTN_W), lambda bi, qi: (bi, qi, 0)),
        out_shape=jax.ShapeDtypeStruct((b, s, ATTN_W), BF16),
        scratch_shapes=[
            pltpu.VMEM((q_per_step, n_chains, V_DIM, t), BF16),
            pltpu.VMEM((q_per_step, n_chains, t), F32),
            pltpu.VMEM((q_per_step, n_chains, V_DIM + L_ROWS, t), F32),
            pltpu.VMEM((2, t, t), F32),
            pltpu.VMEM((t // 2, t // 2), F32),
        ],
        compiler_params=pltpu.CompilerParams(
            dimension_semantics=("arbitrary", "arbitrary"), vmem_limit_bytes=48 * MIB),
        name="diffattn",
    )(qt, k, vt, lq1, lk1, lq2, lk2, g_col)


def _layer_norm(v, g, b):
    mu = jnp.mean(v, axis=-1, keepdims=True)
    c = v - mu
    var = jnp.mean(c * c, axis=-1, keepdims=True)
    return c * lax.rsqrt(var + LN_EPS) * g + b


def _tail_kernel(x_ref, attn_ref, conv_ref, wo_ref, ln1g_ref, ln1b_ref, wup_ref, fcw_ref, fcb_ref,
                 wdn_ref, ln2g_ref, ln2b_ref, o_ref, gate_ref, val_ref, ghist_ref, h1_ref, h1b_ref, act_ref,
                 *, rows, d_ff, chunks):
    @pl.when(pl.program_id(1) == 0)
    def _():
        ghist_ref[...] = jnp.zeros(ghist_ref.shape, F32)

    pieces = tuple(pl.ds(r, SUB_ROWS // 2) for r in range(0, rows, SUB_ROWS // 2))
    for rs in pieces:
        mix = (jnp.dot(attn_ref[rs, :], wo_ref[:ATTN_W, :], preferred_element_type=F32)
               + jnp.dot(conv_ref[rs, :], wo_ref[ATTN_W:, :], preferred_element_type=F32))
        h1 = _layer_norm(DEEPNORM_ALPHA * x_ref[rs, :] + mix, ln1g_ref[...], ln1b_ref[...])
        h1_ref[rs, :] = h1
        h1b_ref[rs, :] = h1.astype(BF16)

    for sub in range(rows // SUB_ROWS):
        sub_rows = pl.ds(sub * SUB_ROWS, SUB_ROWS)
        act = act_ref.at[sub % 2]
        for n, (c0, cw) in enumerate(chunks):
            cols = slice(c0, c0 + cw)
            gate = gate_ref.at[n % 2, :, 0:cw]
            val = val_ref.at[n % 2, :, 0:cw]
            gate[...] = jnp.dot(h1b_ref[sub_rows, :], wup_ref[:, cols], preferred_element_type=F32)
            val[...] = jnp.dot(h1b_ref[sub_rows, :], wup_ref[:, d_ff + c0:d_ff + c0 + cw],
                               preferred_element_type=F32)
            gc = _causal_conv3(gate[...], ghist_ref[:, cols], fcw_ref[:, cols]) + fcb_ref[:, cols]
            ghist_ref[:, cols] = gate[pl.ds(SUB_ROWS - SUBLANES, SUBLANES), :]
            act[:, cols] = (gc * jax.nn.sigmoid(gc) * val[...]).astype(BF16)
        for half in range(2):
            local = pl.ds(half * (SUB_ROWS // 2), SUB_ROWS // 2)
            rs = pieces[2 * sub + half]
            ffn = jnp.dot(act[local, :], wdn_ref[...], preferred_element_type=F32)
            o_ref[rs, :] = _layer_norm(DEEPNORM_ALPHA * h1_ref[rs, :] + ffn, ln2g_ref[...], ln2b_ref[...])


def _tail(x, attn, conv, wo, ln1g, ln1b, wup, fcw, fcb, wdn, ln2g, ln2b, *, rows, chunk):
    b, s, d = x.shape
    d_ff = wdn.shape[0]
    chunks = tuple((c0, min(chunk, d_ff - c0)) for c0 in range(0, d_ff, chunk))
    half_blk = pl.BlockSpec((None, rows, ATTN_W), lambda bi, ti: (bi, ti, 0))
    full_blk = pl.BlockSpec((None, rows, d), lambda bi, ti: (bi, ti, 0))
    return pl.pallas_call(
        functools.partial(_tail_kernel, rows=rows, d_ff=d_ff, chunks=chunks),
        grid=(b, s // rows),
        in_specs=[full_blk, half_blk, half_blk,
                  _resident(wo.shape), _resident(ln1g.shape), _resident(ln1b.shape),
                  _resident(wup.shape), _resident(fcw.shape), _resident(fcb.shape), _resident(wdn.shape),
                  _resident(ln2g.shape), _resident(ln2b.shape)],
        out_specs=full_blk,
        out_shape=jax.ShapeDtypeStruct((b, s, d), x.dtype),
        scratch_shapes=[pltpu.VMEM((2, SUB_ROWS, chunk), F32),
                        pltpu.VMEM((2, SUB_ROWS, chunk), F32),
                        pltpu.VMEM((SUBLANES, d_ff), F32),
                        pltpu.VMEM((rows, d), F32),
                        pltpu.VMEM((rows, d), BF16),
                        pltpu.VMEM((2, SUB_ROWS, d_ff), BF16)],
        compiler_params=pltpu.CompilerParams(
            dimension_semantics=("arbitrary", "arbitrary"), vmem_limit_bytes=56 * MIB),
        name="tail",
    )(x, attn, conv, wo, ln1g, ln1b, wup, fcw, fcb, wdn, ln2g, ln2b)


def kernel(x, w_in, lambda_q1, lambda_k1, lambda_q2, lambda_k2, attn_norm_g, conv_w, w_out, ln1_g, ln1_b,
           ffn_w_up, ffn_conv_w, ffn_conv_b, ffn_w_down, ln2_g, ln2_b):
    win = w_in[0].astype(BF16)

    k, qt, vt, conv = _inproj(x, win, conv_w[0], rows=512)
    attn = _attention(qt, k, vt, lambda_q1, lambda_k1, lambda_q2, lambda_k2,
                      attn_norm_g[0].reshape(V_DIM, 1), t=512, q_per_step=2)
    return _tail(x, attn, conv, w_out[0].astype(BF16), ln1_g, ln1_b, ffn_w_up[0].astype(BF16),
                 ffn_conv_w[0], ffn_conv_b, ffn_w_down[0].astype(BF16), ln2_g, ln2_b, rows=1024, chunk=512)
```

```python
import functools
import math

import jax
import jax.numpy as jnp
from jax import lax
from jax.experimental import pallas as pl
from jax.experimental.pallas import tpu as pltpu

F32 = jnp.float32
BF16 = jnp.bfloat16

N_HEADS = 4
QK_DIM = 64
V_DIM = 2 * QK_DIM
ATTN_W = N_HEADS * V_DIM
CONV_K = 3
LN_EPS = 1e-5
RMS_EPS = 1e-5
LAMBDA_INIT = 0.8 - 0.6 * math.exp(-0.3 * 0)
DEEPNORM_ALPHA = 2.0 ** 0.25
SCORE_SCALE = QK_DIM ** -0.5 * math.log2(math.e)

L_ROWS = 16
SUB_ROWS = 512
SUBLANES = 8
MIB = 1024 * 1024


def _resident(shape):
    return pl.BlockSpec(shape, lambda *_: (0,) * len(shape), pipeline_mode=pl.Buffered(1))


def _causal_conv3(cur, hist, w):
    row = lax.broadcasted_iota(jnp.int32, hist.shape, 0)

    def shifted(s):
        rolled = pltpu.roll(cur, s, axis=0)
        head = jnp.where(row < s, pltpu.roll(hist, s, axis=0), rolled[:SUBLANES])
        return jnp.concatenate([head, rolled[SUBLANES:]], axis=0)

    return w[0:1, :] * shifted(2) + w[1:2, :] * shifted(1) + w[2:3, :] * cur


def _inproj_kernel(x_ref, win_ref, convw_ref, k_ref, qt_ref, vt_ref, conv_ref, uhist_ref, gates_ref,
                   *, rows):
    @pl.when(pl.program_id(1) == 0)
    def _():
        uhist_ref[...] = jnp.zeros(uhist_ref.shape, F32)

    xb = x_ref[...].astype(BF16)
    q = jnp.dot(xb, win_ref[:, :ATTN_W], preferred_element_type=F32) * SCORE_SCALE
    qt_ref[...] = q.astype(BF16).T
    v = jnp.dot(xb, win_ref[:, 2 * ATTN_W:3 * ATTN_W], preferred_element_type=F32)
    vt_ref[...] = v.astype(BF16).T
    k_ref[...] = jnp.dot(xb, win_ref[:, ATTN_W:2 * ATTN_W], preferred_element_type=F32).astype(BF16)

    half = ATTN_W // 2
    for c0 in (0, half):
        for part in range(3):
            src = 3 * ATTN_W + part * ATTN_W + c0
            gates_ref[:, part * ATTN_W + c0:part * ATTN_W + c0 + half] = jnp.dot(
                xb, win_ref[:, src:src + half], preferred_element_type=F32)
    for c0 in (0, half):
        cols = slice(c0, c0 + half)
        u = gates_ref[:, ATTN_W + c0:ATTN_W + c0 + half] * gates_ref[:, 2 * ATTN_W + c0:2 * ATTN_W + c0 + half]
        conv = _causal_conv3(u, uhist_ref[:, cols], convw_ref[:, cols])
        conv_ref[:, cols] = (gates_ref[:, cols] * conv).astype(BF16)
        uhist_ref[:, cols] = u[rows - SUBLANES:]


def _inproj(x, win, conv_w, *, rows):
    b, s, d = x.shape
    grid = (b, s // rows)
    row_blk = pl.BlockSpec((None, rows, ATTN_W), lambda bi, ti: (bi, ti, 0))
    col_blk = pl.BlockSpec((None, ATTN_W, rows), lambda bi, ti: (bi, 0, ti))
    return pl.pallas_call(
        functools.partial(_inproj_kernel, rows=rows),
        grid=grid,
        in_specs=[
            pl.BlockSpec((None, rows, d), lambda bi, ti: (bi, ti, 0)),
            _resident(win.shape),
            _resident(conv_w.shape),
        ],
        out_specs=[row_blk, col_blk, col_blk, row_blk],
        out_shape=[
            jax.ShapeDtypeStruct((b, s, ATTN_W), BF16),
            jax.ShapeDtypeStruct((b, ATTN_W, s), BF16),
            jax.ShapeDtypeStruct((b, ATTN_W, s), BF16),
            jax.ShapeDtypeStruct((b, s, ATTN_W), BF16),
        ],
        scratch_shapes=[pltpu.VMEM((SUBLANES, ATTN_W), F32),
                        pltpu.VMEM((rows, 3 * ATTN_W), F32)],
        compiler_params=pltpu.CompilerParams(
            dimension_semantics=("arbitrary", "arbitrary"), vmem_limit_bytes=40 * MIB),
        name="inproj",
    )(x, win, conv_w)


def _attn_kernel(qt_ref, k_ref, vt_ref, lq1_ref, lk1_ref, lq2_ref, lk2_ref, g_ref, w1_ref, w2_ref, w3_ref,
                 o_ref, w1b_ref, w2b_ref, w3b_ref, qpad_ref, m_ref, acc_ref, s_ref, bias_ref, *, t, q_per_step):
    for w_ref, wb_ref in ((w1_ref, w1b_ref), (w2_ref, w2b_ref), (w3_ref, w3b_ref)):
        wb_ref[...] = w_ref[...].astype(BF16)

    n_chains = 2 * N_HEADS
    hq = t // 2
    left, right = slice(0, hq), slice(hq, t)
    top_keys, bot_keys = slice(0, hq), slice(hq, t)

    @pl.when(jnp.logical_and(pl.program_id(0) == 0, pl.program_id(1) == 0))
    def _():
        kpos = lax.broadcasted_iota(jnp.int32, (hq, hq), 0)
        qpos = lax.broadcasted_iota(jnp.int32, (hq, hq), 1)
        bias_ref[...] = jnp.where(kpos <= qpos, 0.0, -jnp.inf).astype(F32)

    lam = (jnp.exp(jnp.sum(lq1_ref[...] * lk1_ref[...], keepdims=True))
           - jnp.exp(jnp.sum(lq2_ref[...] * lk2_ref[...], keepdims=True)) + LAMBDA_INIT)

    def query_tile(u):
        qi = pl.program_id(1) * q_per_step + u
        qcols = slice(u * t, (u + 1) * t)
        qpad, m, acc = qpad_ref.at[u], m_ref.at[u], acc_ref.at[u]
        top = lax.broadcasted_iota(jnp.int32, (V_DIM, t), 0) < QK_DIM
        for h in range(N_HEADS):
            qt = qt_ref[h * V_DIM:(h + 1) * V_DIM, qcols]
            zero = jnp.zeros_like(qt)
            qpad[2 * h] = jnp.where(top, qt, zero)
            qpad[2 * h + 1] = jnp.where(top, zero, qt)

        def softmax_pv(c, s_cols, keys, vt_cols, m_new):
            h = c // 2
            p = jnp.exp2(s_ref[c % 2, keys, s_cols] - m_new).astype(BF16)
            vtblk = vt_ref[h * V_DIM:(h + 1) * V_DIM, vt_cols]
            ones_rows = jnp.ones((L_ROWS, vtblk.shape[1]), BF16)
            return jnp.dot(jnp.concatenate([vtblk, ones_rows], axis=0), p, preferred_element_type=F32)

        def scores(start, c):
            h = c // 2
            kblk = k_ref[pl.ds(start, t), h * V_DIM:(h + 1) * V_DIM]
            s_ref[c % 2] = jnp.dot(kblk, qpad[c], preferred_element_type=F32)

        def diag_scores(start, c):
            h = c // 2
            ktop = k_ref[pl.ds(start, hq), h * V_DIM:(h + 1) * V_DIM]
            kbot = k_ref[pl.ds(pl.multiple_of(start + hq, hq), hq), h * V_DIM:(h + 1) * V_DIM]
            s_ref[c % 2, top_keys, :] = jnp.dot(ktop, qpad[c], preferred_element_type=F32)
            s_ref[c % 2, bot_keys, right] = jnp.dot(kbot, qpad[c, :, right], preferred_element_type=F32)

        def diag_tile(start):
            for c in range(n_chains):
                if c + 1 < n_chains:
                    diag_scores(start, c + 1)
                else:
                    scores(0, 0)
                slot = c % 2
                s_ref[slot, top_keys, left] = s_ref[slot, top_keys, left] + bias_ref[...]
                s_ref[slot, bot_keys, right] = s_ref[slot, bot_keys, right] + bias_ref[...]
                m_top = jnp.max(s_ref[slot, top_keys, :], axis=0, keepdims=True)
                m_bot = jnp.max(s_ref[slot, bot_keys, right], axis=0, keepdims=True)
                m_new = jnp.concatenate([m_top[:, left], jnp.maximum(m_top[:, right], m_bot)], axis=1)
                m[c:c + 1, :] = m_new
                pv = softmax_pv(c, slice(None), top_keys, pl.ds(start, hq), m_new)
                pv_r = softmax_pv(c, right, bot_keys, pl.ds(pl.multiple_of(start + hq, hq), hq), m_new[:, right])
                acc[c, :, left] = pv[:, left]
                acc[c, :, right] = pv[:, right] + pv_r

        def full_tile(j, carry):
            start = pl.multiple_of(j * t, t)
            for c in range(n_chains):
                if c + 1 < n_chains:
                    scores(start, c + 1)
                else:
                    scores(pl.multiple_of(start + t, t), 0)
                m_old = m[c:c + 1, :]
                m_new = jnp.maximum(m_old, jnp.max(s_ref[c % 2], axis=0, keepdims=True))
                m[c:c + 1, :] = m_new
                pv = softmax_pv(c, slice(None), slice(None), pl.ds(start, t), m_new)
                acc[c] = jnp.exp2(m_old - m_new) * acc[c] + pv
            return carry

        diag_start = pl.multiple_of(qi * t, t)
        diag_scores(diag_start, 0)
        diag_tile(diag_start)
        lax.fori_loop(0, qi, full_tile, 0)

        for h in range(N_HEADS):
            c1, c2 = 2 * h, 2 * h + 1
            o = (acc[c1, :V_DIM, :] / acc[c1, V_DIM:V_DIM + 1, :]
                 - lam * (acc[c2, :V_DIM, :] / acc[c2, V_DIM:V_DIM + 1, :]))
            o = o * lax.rsqrt(jnp.mean(o * o, axis=0, keepdims=True) + RMS_EPS)
            o = o * g_ref[...] * (1.0 - LAMBDA_INIT)
            o_ref[qcols, h * V_DIM:(h + 1) * V_DIM] = o.T.astype(BF16)

    for u in range(q_per_step):
        query_tile(u)


def _attention(qt, k, vt, lq1, lk1, lq2, lk2, g_col, weights, *, t, q_per_step):
    b, s, _ = k.shape
    n_chains = 2 * N_HEADS
    tq = q_per_step * t
    steps_per_batch = s // tq
    lam_blk = pl.BlockSpec((1, QK_DIM), lambda bi, qi: (0, 0))
    slabs = [pl.BlockSpec((w.shape[0] // (b * steps_per_batch), w.shape[1]),
                          lambda bi, qi: (bi * steps_per_batch + qi, 0)) for w in weights]
    return pl.pallas_call(
        functools.partial(_attn_kernel, t=t, q_per_step=q_per_step),
        grid=(b, s // tq),
        in_specs=[
            pl.BlockSpec((None, ATTN_W, tq), lambda bi, qi: (bi, 0, qi)),
            pl.BlockSpec((None, s, ATTN_W), lambda bi, qi: (bi, 0, 0)),
            pl.BlockSpec((None, ATTN_W, s), lambda bi, qi: (bi, 0, 0)),
            lam_blk, lam_blk, lam_blk, lam_blk,
            pl.BlockSpec((V_DIM, 1), lambda bi, qi: (0, 0)),
            *slabs,
        ],
        out_specs=[pl.BlockSpec((None, tq, ATTN_W), lambda bi, qi: (bi, qi, 0)), *slabs],
        out_shape=[jax.ShapeDtypeStruct((b, s, ATTN_W), BF16),
                   *[jax.ShapeDtypeStruct(w.shape, BF16) for w in weights]],
        scratch_shapes=[
            pltpu.VMEM((q_per_step, n_chains, V_DIM, t), BF16),
            pltpu.VMEM((q_per_step, n_chains, t), F32),
            pltpu.VMEM((q_per_step, n_chains, V_DIM + L_ROWS, t), F32),
            pltpu.VMEM((2, t, t), F32),
            pltpu.VMEM((t // 2, t // 2), F32),
        ],
        compiler_params=pltpu.CompilerParams(
            dimension_semantics=("arbitrary", "arbitrary"), vmem_limit_bytes=48 * MIB),
        name="diffattn",
    )(qt, k, vt, lq1, lk1, lq2, lk2, g_col, *weights)


def _layer_norm(v, g, b):
    mu = jnp.mean(v, axis=-1, keepdims=True)
    c = v - mu
    var = jnp.mean(c * c, axis=-1, keepdims=True)
    return c * lax.rsqrt(var + LN_EPS) * g + b


def _tail_kernel(x_ref, attn_ref, conv_ref, wo_ref, ln1g_ref, ln1b_ref, wup_ref, fcw_ref, fcb_ref,
                 wdn_ref, ln2g_ref, ln2b_ref, o_ref, gate_ref, val_ref, ghist_ref, h1_ref, h1b_ref, act_ref,
                 *, rows, d_ff, chunks):
    @pl.when(pl.program_id(1) == 0)
    def _():
        ghist_ref[...] = jnp.zeros(ghist_ref.shape, F32)

    pieces = tuple(pl.ds(r, SUB_ROWS // 2) for r in range(0, rows, SUB_ROWS // 2))
    for rs in pieces:
        mix = (jnp.dot(attn_ref[rs, :], wo_ref[:ATTN_W, :], preferred_element_type=F32)
               + jnp.dot(conv_ref[rs, :], wo_ref[ATTN_W:, :], preferred_element_type=F32))
        h1 = _layer_norm(DEEPNORM_ALPHA * x_ref[rs, :] + mix, ln1g_ref[...], ln1b_ref[...])
        h1_ref[rs, :] = h1
        h1b_ref[rs, :] = h1.astype(BF16)

    for sub in range(rows // SUB_ROWS):
        sub_rows = pl.ds(sub * SUB_ROWS, SUB_ROWS)
        act = act_ref.at[sub % 2]
        for n, (c0, cw) in enumerate(chunks):
            cols = slice(c0, c0 + cw)
            gate = gate_ref.at[n % 2, :, 0:cw]
            val = val_ref.at[n % 2, :, 0:cw]
            gate[...] = jnp.dot(h1b_ref[sub_rows, :], wup_ref[:, cols], preferred_element_type=F32)
            val[...] = jnp.dot(h1b_ref[sub_rows, :], wup_ref[:, d_ff + c0:d_ff + c0 + cw],
                               preferred_element_type=F32)
            gc = _causal_conv3(gate[...], ghist_ref[:, cols], fcw_ref[:, cols]) + fcb_ref[:, cols]
            ghist_ref[:, cols] = gate[pl.ds(SUB_ROWS - SUBLANES, SUBLANES), :]
            act[:, cols] = (gc * jax.nn.sigmoid(gc) * val[...]).astype(BF16)
        for half in range(2):
            local = pl.ds(half * (SUB_ROWS // 2), SUB_ROWS // 2)
            rs = pieces[2 * sub + half]
            ffn = jnp.dot(act[local, :], wdn_ref[...], preferred_element_type=F32)
            o_ref[rs, :] = _layer_norm(DEEPNORM_ALPHA * h1_ref[rs, :] + ffn, ln2g_ref[...], ln2b_ref[...])


def _tail(x, attn, conv, wo, ln1g, ln1b, wup, fcw, fcb, wdn, ln2g, ln2b, *, rows, chunk):
    b, s, d = x.shape
    d_ff = wdn.shape[0]
    chunks = tuple((c0, min(chunk, d_ff - c0)) for c0 in range(0, d_ff, chunk))
    half_blk = pl.BlockSpec((None, rows, ATTN_W), lambda bi, ti: (bi, ti, 0))
    full_blk = pl.BlockSpec((None, rows, d), lambda bi, ti: (bi, ti, 0))
    return pl.pallas_call(
        functools.partial(_tail_kernel, rows=rows, d_ff=d_ff, chunks=chunks),
        grid=(b, s // rows),
        in_specs=[full_blk, half_blk, half_blk,
                  _resident(wo.shape), _resident(ln1g.shape), _resident(ln1b.shape),
                  _resident(wup.shape), _resident(fcw.shape), _resident(fcb.shape), _resident(wdn.shape),
                  _resident(ln2g.shape), _resident(ln2b.shape)],
        out_specs=full_blk,
        out_shape=jax.ShapeDtypeStruct((b, s, d), x.dtype),
        scratch_shapes=[pltpu.VMEM((2, SUB_ROWS, chunk), F32),
                        pltpu.VMEM((2, SUB_ROWS, chunk), F32),
                        pltpu.VMEM((SUBLANES, d_ff), F32),
                        pltpu.VMEM((rows, d), F32),
                        pltpu.VMEM((rows, d), BF16),
                        pltpu.VMEM((2, SUB_ROWS, d_ff), BF16)],
        compiler_params=pltpu.CompilerParams(
            dimension_semantics=("arbitrary", "arbitrary"), vmem_limit_bytes=56 * MIB),
        name="tail",
    )(x, attn, conv, wo, ln1g, ln1b, wup, fcw, fcb, wdn, ln2g, ln2b)


def kernel(x, w_in, lambda_q1, lambda_k1, lambda_q2, lambda_k2, attn_norm_g, conv_w, w_out, ln1_g, ln1_b,
           ffn_w_up, ffn_conv_w, ffn_conv_b, ffn_w_down, ln2_g, ln2_b):
    win = w_in[0].astype(BF16)

    k, qt, vt, conv = _inproj(x, win, conv_w[0], rows=1024)
    attn, wo, wup, wdn = _attention(qt, k, vt, lambda_q1, lambda_k1, lambda_q2, lambda_k2,
                                    attn_norm_g[0].reshape(V_DIM, 1), (w_out[0], ffn_w_up[0], ffn_w_down[0]),
                                    t=512, q_per_step=2)
    return _tail(x, attn, conv, wo, ln1_g, ln1_b, wup, ffn_conv_w[0], ffn_conv_b, wdn, ln2_g, ln2_b,
                 rows=1024, chunk=512)
```

```python
import functools
import math

import jax
import jax.numpy as jnp
from jax import lax
from jax.experimental import pallas as pl
from jax.experimental.pallas import tpu as pltpu

F32 = jnp.float32
BF16 = jnp.bfloat16

N_HEADS = 4
QK_DIM = 64
V_DIM = 2 * QK_DIM
ATTN_W = N_HEADS * V_DIM
LN_EPS = 1e-5
RMS_EPS = 1e-5
LAMBDA_INIT = 0.8 - 0.6 * math.exp(-0.3 * 0)
DEEPNORM_ALPHA = 2.0 ** 0.25
SCORE_SCALE = QK_DIM ** -0.5 * math.log2(math.e)

SUBLANES = 8
L_ROWS = 16
MIB = 1024 * 1024

INPROJ_ROWS = 1024
ATTN_TILE = 512
ATTN_Q_PER_STEP = 2
LOOKAHEAD = 2
S_SLOTS = 4
TAIL_ROWS = 1024
SUB_ROWS = 512
FFN_CHUNK = 256
VMEM_LIMIT = {"inproj": 40 * MIB, "diffattn": 48 * MIB, "tail": 56 * MIB}


def _resident(shape):
    return pl.BlockSpec(shape, lambda *_: (0,) * len(shape), pipeline_mode=pl.Buffered(1))


def _causal_conv3(cur, hist, w):
    row = lax.broadcasted_iota(jnp.int32, hist.shape, 0)

    def shifted(s):
        rolled = pltpu.roll(cur, s, axis=0)
        head = jnp.where(row < s, pltpu.roll(hist, s, axis=0), rolled[:SUBLANES])
        return jnp.concatenate([head, rolled[SUBLANES:]], axis=0)

    return w[0:1, :] * shifted(2) + w[1:2, :] * shifted(1) + w[2:3, :] * cur


def _inproj_kernel(x_ref, win_ref, convw_ref, k_ref, qt_ref, vt_ref, conv_ref, uhist_ref, gates_ref,
                   *, rows):
    @pl.when(pl.program_id(1) == 0)
    def _():
        uhist_ref[...] = jnp.zeros(uhist_ref.shape, F32)

    xb = x_ref[...].astype(BF16)
    q = jnp.dot(xb, win_ref[:, :ATTN_W], preferred_element_type=F32) * SCORE_SCALE
    qt_ref[...] = q.astype(BF16).T
    v = jnp.dot(xb, win_ref[:, 2 * ATTN_W:3 * ATTN_W], preferred_element_type=F32)
    vt_ref[...] = v.astype(BF16).T
    k_ref[...] = jnp.dot(xb, win_ref[:, ATTN_W:2 * ATTN_W], preferred_element_type=F32).astype(BF16)

    half = ATTN_W // 2
    for c0 in (0, half):
        for part in range(3):
            src = 3 * ATTN_W + part * ATTN_W + c0
            gates_ref[:, part * ATTN_W + c0:part * ATTN_W + c0 + half] = jnp.dot(
                xb, win_ref[:, src:src + half], preferred_element_type=F32)
    for c0 in (0, half):
        cols = slice(c0, c0 + half)
        u = gates_ref[:, ATTN_W + c0:ATTN_W + c0 + half] * gates_ref[:, 2 * ATTN_W + c0:2 * ATTN_W + c0 + half]
        conv = _causal_conv3(u, uhist_ref[:, cols], convw_ref[:, cols])
        conv_ref[:, cols] = (gates_ref[:, cols] * conv).astype(BF16)
        uhist_ref[:, cols] = u[rows - SUBLANES:]


def _inproj(x, win, conv_w, *, rows):
    b, s, d = x.shape
    grid = (b, s // rows)
    row_blk = pl.BlockSpec((None, rows, ATTN_W), lambda bi, ti: (bi, ti, 0))
    col_blk = pl.BlockSpec((None, ATTN_W, rows), lambda bi, ti: (bi, 0, ti))
    return pl.pallas_call(
        functools.partial(_inproj_kernel, rows=rows),
        grid=grid,
        in_specs=[
            pl.BlockSpec((None, rows, d), lambda bi, ti: (bi, ti, 0)),
            _resident(win.shape),
            _resident(conv_w.shape),
        ],
        out_specs=[row_blk, col_blk, col_blk, row_blk],
        out_shape=[
            jax.ShapeDtypeStruct((b, s, ATTN_W), BF16),
            jax.ShapeDtypeStruct((b, ATTN_W, s), BF16),
            jax.ShapeDtypeStruct((b, ATTN_W, s), BF16),
            jax.ShapeDtypeStruct((b, s, ATTN_W), BF16),
        ],
        scratch_shapes=[pltpu.VMEM((SUBLANES, ATTN_W), F32),
                        pltpu.VMEM((rows, 3 * ATTN_W), F32)],
        compiler_params=pltpu.CompilerParams(
            dimension_semantics=("arbitrary", "arbitrary"), vmem_limit_bytes=VMEM_LIMIT["inproj"]),
        name="inproj",
    )(x, win, conv_w)


def _attn_kernel(qt_ref, k_ref, vt_ref, lq1_ref, lk1_ref, lq2_ref, lk2_ref, g_ref, w1_ref, w2_ref, w3_ref,
                 o_ref, w1b_ref, w2b_ref, w3b_ref, qpad_ref, m_ref, acc_ref, s_ref, bias_ref, *, t, q_per_step):
    for w_ref, wb_ref in ((w1_ref, w1b_ref), (w2_ref, w2b_ref), (w3_ref, w3b_ref)):
        wb_ref[...] = w_ref[...].astype(BF16)

    n_chains = 2 * N_HEADS
    hq = t // 2
    left, right = slice(0, hq), slice(hq, t)
    top_keys, bot_keys = slice(0, hq), slice(hq, t)

    @pl.when(jnp.logical_and(pl.program_id(0) == 0, pl.program_id(1) == 0))
    def _():
        kpos = lax.broadcasted_iota(jnp.int32, (hq, hq), 0)
        qpos = lax.broadcasted_iota(jnp.int32, (hq, hq), 1)
        bias_ref[...] = jnp.where(kpos <= qpos, 0.0, -jnp.inf).astype(F32)

    lam = (jnp.exp(jnp.sum(lq1_ref[...] * lk1_ref[...], keepdims=True))
           - jnp.exp(jnp.sum(lq2_ref[...] * lk2_ref[...], keepdims=True)) + LAMBDA_INIT)

    def query_tile(u):
        qi = pl.program_id(1) * q_per_step + u
        qcols = slice(u * t, (u + 1) * t)
        qpad, m, acc = qpad_ref.at[u], m_ref.at[u], acc_ref.at[u]
        top = lax.broadcasted_iota(jnp.int32, (V_DIM, t), 0) < QK_DIM
        for h in range(N_HEADS):
            qt = qt_ref[h * V_DIM:(h + 1) * V_DIM, qcols]
            zero = jnp.zeros_like(qt)
            qpad[2 * h] = jnp.where(top, qt, zero)
            qpad[2 * h + 1] = jnp.where(top, zero, qt)

        def softmax_pv(c, s_cols, keys, vt_cols, m_new):
            h = c // 2
            p = jnp.exp2(s_ref[c % S_SLOTS, keys, s_cols] - m_new).astype(BF16)
            vtblk = vt_ref[h * V_DIM:(h + 1) * V_DIM, vt_cols]
            ones_rows = jnp.ones((L_ROWS, vtblk.shape[1]), BF16)
            return jnp.dot(jnp.concatenate([vtblk, ones_rows], axis=0), p, preferred_element_type=F32)

        def scores(start, c):
            h = c // 2
            kblk = k_ref[pl.ds(start, t), h * V_DIM:(h + 1) * V_DIM]
            s_ref[c % S_SLOTS] = jnp.dot(kblk, qpad[c], preferred_element_type=F32)

        def diag_scores(start, c):
            h = c // 2
            ktop = k_ref[pl.ds(start, hq), h * V_DIM:(h + 1) * V_DIM]
            kbot = k_ref[pl.ds(pl.multiple_of(start + hq, hq), hq), h * V_DIM:(h + 1) * V_DIM]
            s_ref[c % S_SLOTS, top_keys, :] = jnp.dot(ktop, qpad[c], preferred_element_type=F32)
            s_ref[c % S_SLOTS, bot_keys, right] = jnp.dot(kbot, qpad[c, :, right], preferred_element_type=F32)

        def diag_tile(start):
            for c in range(n_chains):
                ahead = c + LOOKAHEAD
                if ahead < n_chains:
                    diag_scores(start, ahead)
                else:
                    scores(0, ahead - n_chains)
                slot = c % S_SLOTS
                s_ref[slot, top_keys, left] = s_ref[slot, top_keys, left] + bias_ref[...]
                s_ref[slot, bot_keys, right] = s_ref[slot, bot_keys, right] + bias_ref[...]
                m_top = jnp.max(s_ref[slot, top_keys, :], axis=0, keepdims=True)
                m_bot = jnp.max(s_ref[slot, bot_keys, right], axis=0, keepdims=True)
                m_new = jnp.concatenate([m_top[:, left], jnp.maximum(m_top[:, right], m_bot)], axis=1)
                m[c:c + 1, :] = m_new
                pv = softmax_pv(c, slice(None), top_keys, pl.ds(start, hq), m_new)
                pv_r = softmax_pv(c, right, bot_keys, pl.ds(pl.multiple_of(start + hq, hq), hq), m_new[:, right])
                acc[c, :, left] = pv[:, left]
                acc[c, :, right] = pv[:, right] + pv_r

        def full_tile(j, carry):
            start = pl.multiple_of(j * t, t)
            for c in range(n_chains):
                ahead = c + LOOKAHEAD
                if ahead < n_chains:
                    scores(start, ahead)
                else:
                    scores(pl.multiple_of(start + t, t), ahead - n_chains)
                m_old = m[c:c + 1, :]
                m_new = jnp.maximum(m_old, jnp.max(s_ref[c % S_SLOTS], axis=0, keepdims=True))
                m[c:c + 1, :] = m_new
                pv = softmax_pv(c, slice(None), slice(None), pl.ds(start, t), m_new)
                acc[c] = jnp.exp2(m_old - m_new) * acc[c] + pv
            return carry

        diag_start = pl.multiple_of(qi * t, t)
        for c in range(LOOKAHEAD):
            diag_scores(diag_start, c)
        diag_tile(diag_start)
        lax.fori_loop(0, qi, full_tile, 0)

        for h in range(N_HEADS):
            c1, c2 = 2 * h, 2 * h + 1
            o = (acc[c1, :V_DIM, :] / acc[c1, V_DIM:V_DIM + 1, :]
                 - lam * (acc[c2, :V_DIM, :] / acc[c2, V_DIM:V_DIM + 1, :]))
            o = o * lax.rsqrt(jnp.mean(o * o, axis=0, keepdims=True) + RMS_EPS)
            o = o * g_ref[...] * (1.0 - LAMBDA_INIT)
            o_ref[qcols, h * V_DIM:(h + 1) * V_DIM] = o.T.astype(BF16)

    for u in range(q_per_step):
        query_tile(u)


def _attention(qt, k, vt, lq1, lk1, lq2, lk2, g_col, weights, *, t, q_per_step):
    b, s, _ = k.shape
    n_chains = 2 * N_HEADS
    tq = q_per_step * t
    steps_per_batch = s // tq
    lam_blk = pl.BlockSpec((1, QK_DIM), lambda bi, qi: (0, 0))
    slabs = [pl.BlockSpec((w.shape[0] // (b * steps_per_batch), w.shape[1]),
                          lambda bi, qi: (bi * steps_per_batch + qi, 0)) for w in weights]
    return pl.pallas_call(
        functools.partial(_attn_kernel, t=t, q_per_step=q_per_step),
        grid=(b, s // tq),
        in_specs=[
            pl.BlockSpec((None, ATTN_W, tq), lambda bi, qi: (bi, 0, qi)),
            pl.BlockSpec((None, s, ATTN_W), lambda bi, qi: (bi, 0, 0)),
            pl.BlockSpec((None, ATTN_W, s), lambda bi, qi: (bi, 0, 0)),
            lam_blk, lam_blk, lam_blk, lam_blk,
            pl.BlockSpec((V_DIM, 1), lambda bi, qi: (0, 0)),
            *slabs,
        ],
        out_specs=[pl.BlockSpec((None, tq, ATTN_W), lambda bi, qi: (bi, qi, 0)), *slabs],
        out_shape=[jax.ShapeDtypeStruct((b, s, ATTN_W), BF16),
                   *[jax.ShapeDtypeStruct(w.shape, BF16) for w in weights]],
        scratch_shapes=[
            pltpu.VMEM((q_per_step, n_chains, V_DIM, t), BF16),
            pltpu.VMEM((q_per_step, n_chains, t), F32),
            pltpu.VMEM((q_per_step, n_chains, V_DIM + L_ROWS, t), F32),
            pltpu.VMEM((S_SLOTS, t, t), F32),
            pltpu.VMEM((t // 2, t // 2), F32),
        ],
        compiler_params=pltpu.CompilerParams(
            dimension_semantics=("arbitrary", "arbitrary"), vmem_limit_bytes=VMEM_LIMIT["diffattn"]),
        name="diffattn",
    )(qt, k, vt, lq1, lk1, lq2, lk2, g_col, *weights)


def _layer_norm(v, g, b):
    mu = jnp.mean(v, axis=-1, keepdims=True)
    c = v - mu
    var = jnp.mean(c * c, axis=-1, keepdims=True)
    return c * lax.rsqrt(var + LN_EPS) * g + b


def _tail_kernel(x_ref, attn_ref, conv_ref, wo_ref, ln1g_ref, ln1b_ref, wup_ref, fcw_ref, fcb_ref,
                 wdn_ref, ln2g_ref, ln2b_ref, o_ref, gate_ref, val_ref, ghist_ref, h1_ref, h1b_ref, act_ref,
                 *, rows, d_ff, chunks):
    @pl.when(pl.program_id(1) == 0)
    def _():
        ghist_ref[...] = jnp.zeros(ghist_ref.shape, F32)

    pieces = tuple(pl.ds(r, SUB_ROWS // 2) for r in range(0, rows, SUB_ROWS // 2))
    for rs in pieces:
        mix = (jnp.dot(attn_ref[rs, :], wo_ref[:ATTN_W, :], preferred_element_type=F32)
               + jnp.dot(conv_ref[rs, :], wo_ref[ATTN_W:, :], preferred_element_type=F32))
        h1 = _layer_norm(DEEPNORM_ALPHA * x_ref[rs, :] + mix, ln1g_ref[...], ln1b_ref[...])
        h1_ref[rs, :] = h1
        h1b_ref[rs, :] = h1.astype(BF16)

    for sub in range(rows // SUB_ROWS):
        sub_rows = pl.ds(sub * SUB_ROWS, SUB_ROWS)
        act = act_ref.at[sub % 2]
        for n, (c0, cw) in enumerate(chunks):
            cols = slice(c0, c0 + cw)
            gate = gate_ref.at[n % 2, :, 0:cw]
            val = val_ref.at[n % 2, :, 0:cw]
            gate[...] = jnp.dot(h1b_ref[sub_rows, :], wup_ref[:, cols], preferred_element_type=F32)
            val[...] = jnp.dot(h1b_ref[sub_rows, :], wup_ref[:, d_ff + c0:d_ff + c0 + cw],
                               preferred_element_type=F32)
            gc = _causal_conv3(gate[...], ghist_ref[:, cols], fcw_ref[:, cols]) + fcb_ref[:, cols]
            ghist_ref[:, cols] = gate[pl.ds(SUB_ROWS - SUBLANES, SUBLANES), :]
            act[:, cols] = (gc * jax.nn.sigmoid(gc) * val[...]).astype(BF16)
        for half in range(2):
            local = pl.ds(half * (SUB_ROWS // 2), SUB_ROWS // 2)
            rs = pieces[2 * sub + half]
            ffn = jnp.dot(act[local, :], wdn_ref[...], preferred_element_type=F32)
            o_ref[rs, :] = _layer_norm(DEEPNORM_ALPHA * h1_ref[rs, :] + ffn, ln2g_ref[...], ln2b_ref[...])


def _tail(x, attn, conv, wo, ln1g, ln1b, wup, fcw, fcb, wdn, ln2g, ln2b, *, rows, chunk):
    b, s, d = x.shape
    d_ff = wdn.shape[0]
    chunks = tuple((c0, min(chunk, d_ff - c0)) for c0 in range(0, d_ff, chunk))
    half_blk = pl.BlockSpec((None, rows, ATTN_W), lambda bi, ti: (bi, ti, 0))
    full_blk = pl.BlockSpec((None, rows, d), lambda bi, ti: (bi, ti, 0))
    return pl.pallas_call(
        functools.partial(_tail_kernel, rows=rows, d_ff=d_ff, chunks=chunks),
        grid=(b, s // rows),
        in_specs=[full_blk, half_blk, half_blk,
                  _resident(wo.shape), _resident(ln1g.shape), _resident(ln1b.shape),
                  _resident(wup.shape), _resident(fcw.shape), _resident(fcb.shape), _resident(wdn.shape),
                  _resident(ln2g.shape), _resident(ln2b.shape)],
        out_specs=full_blk,
        out_shape=jax.ShapeDtypeStruct((b, s, d), x.dtype),
        scratch_shapes=[pltpu.VMEM((2, SUB_ROWS, chunk), F32),
                        pltpu.VMEM((2, SUB_ROWS, chunk), F32),
                        pltpu.VMEM((SUBLANES, d_ff), F32),
                        pltpu.VMEM((rows, d), F32),
                        pltpu.VMEM((rows, d), BF16),
                        pltpu.VMEM((2, SUB_ROWS, d_ff), BF16)],
        compiler_params=pltpu.CompilerParams(
            dimension_semantics=("arbitrary", "arbitrary"), vmem_limit_bytes=VMEM_LIMIT["tail"]),
        name="tail",
    )(x, attn, conv, wo, ln1g, ln1b, wup, fcw, fcb, wdn, ln2g, ln2b)


def kernel(x, w_in, lambda_q1, lambda_k1, lambda_q2, lambda_k2, attn_norm_g, conv_w, w_out, ln1_g, ln1_b,
           ffn_w_up, ffn_conv_w, ffn_conv_b, ffn_w_down, ln2_g, ln2_b):
    win = w_in[0].astype(BF16)

    k, qt, vt, conv = _inproj(x, win, conv_w[0], rows=INPROJ_ROWS)
    attn, wo, wup, wdn = _attention(qt, k, vt, lambda_q1, lambda_k1, lambda_q2, lambda_k2,
                                    attn_norm_g[0].reshape(V_DIM, 1), (w_out[0], ffn_w_up[0], ffn_w_down[0]),
                                    t=ATTN_TILE, q_per_step=ATTN_Q_PER_STEP)
    return _tail(x, attn, conv, wo, ln1_g, ln1_b, wup, ffn_conv_w[0], ffn_conv_b, wdn, ln2_g, ln2_b,
                 rows=TAIL_ROWS, chunk=FFN_CHUNK)
```

```python
import functools
import math

import jax
import jax.numpy as jnp
from jax import lax
from jax.experimental import pallas as pl
from jax.experimental.pallas import tpu as pltpu

F32 = jnp.float32
BF16 = jnp.bfloat16

N_HEADS = 4
QK_DIM = 64
V_DIM = 2 * QK_DIM
ATTN_W = N_HEADS * V_DIM
LN_EPS = 1e-5
RMS_EPS = 1e-5
LAMBDA_INIT = 0.8 - 0.6 * math.exp(-0.3 * 0)
DEEPNORM_ALPHA = 2.0 ** 0.25
SCORE_SCALE = QK_DIM ** -0.5 * math.log2(math.e)

SUBLANES = 8
L_ROWS = 16
MIB = 1024 * 1024

INPROJ_ROWS = 1024
ATTN_TILE = 512
ATTN_Q_PER_STEP = 2
LOOKAHEAD = 2
S_SLOTS = 4
TAIL_ROWS = 1024
SUB_ROWS = 512
FFN_CHUNK = 256
VMEM_LIMIT = {"inproj": 40 * MIB, "diffattn": 48 * MIB, "tail": 56 * MIB}


def _resident(shape):
    block = (None, *shape[1:]) if len(shape) == 3 else shape
    return pl.BlockSpec(block, lambda *_: (0,) * len(shape), pipeline_mode=pl.Buffered(1))


def _causal_conv3(cur, hist, w):
    row = lax.broadcasted_iota(jnp.int32, hist.shape, 0)

    def shifted(s):
        rolled = pltpu.roll(cur, s, axis=0)
        head = jnp.where(row < s, pltpu.roll(hist, s, axis=0), rolled[:SUBLANES])
        return jnp.concatenate([head, rolled[SUBLANES:]], axis=0)

    return w[0:1, :] * shifted(2) + w[1:2, :] * shifted(1) + w[2:3, :] * cur


def _inproj_kernel(x_ref, win_ref, convw_ref, k_ref, qt_ref, vt_ref, conv_ref, uhist_ref, gates_ref,
                   *, rows):
    @pl.when(pl.program_id(1) == 0)
    def _():
        uhist_ref[...] = jnp.zeros(uhist_ref.shape, F32)

    xb = x_ref[...].astype(BF16)
    q = jnp.dot(xb, win_ref[:, :ATTN_W], preferred_element_type=F32) * SCORE_SCALE
    qt_ref[...] = q.astype(BF16).T
    v = jnp.dot(xb, win_ref[:, 2 * ATTN_W:3 * ATTN_W], preferred_element_type=F32)
    vt_ref[...] = v.astype(BF16).T

    half = ATTN_W // 2
    for c0 in (0, half):
        for part in range(3):
            src = 3 * ATTN_W + part * ATTN_W + c0
            gates_ref[:, part * ATTN_W + c0:part * ATTN_W + c0 + half] = jnp.dot(
                xb, win_ref[:, src:src + half], preferred_element_type=F32)
    k_ref[...] = jnp.dot(xb, win_ref[:, ATTN_W:2 * ATTN_W], preferred_element_type=F32).astype(BF16)
    for c0 in (0, half):
        cols = slice(c0, c0 + half)
        u = gates_ref[:, ATTN_W + c0:ATTN_W + c0 + half] * gates_ref[:, 2 * ATTN_W + c0:2 * ATTN_W + c0 + half]
        conv = _causal_conv3(u, uhist_ref[:, cols], convw_ref[:, cols])
        conv_ref[:, cols] = (gates_ref[:, cols] * conv).astype(BF16)
        uhist_ref[:, cols] = u[rows - SUBLANES:]


def _inproj(x, win, conv_w, *, rows):
    b, s, d = x.shape
    assert s % rows == 0 and win.shape == (d, 6 * ATTN_W) and conv_w.shape == (1, 3, ATTN_W)
    grid = (b, s // rows)
    row_blk = pl.BlockSpec((None, rows, ATTN_W), lambda bi, ti: (bi, ti, 0))
    col_blk = pl.BlockSpec((None, ATTN_W, rows), lambda bi, ti: (bi, 0, ti))
    return pl.pallas_call(
        functools.partial(_inproj_kernel, rows=rows),
        grid=grid,
        in_specs=[
            pl.BlockSpec((None, rows, d), lambda bi, ti: (bi, ti, 0)),
            _resident(win.shape),
            _resident(conv_w.shape),
        ],
        out_specs=[row_blk, col_blk, col_blk, row_blk],
        out_shape=[
            jax.ShapeDtypeStruct((b, s, ATTN_W), BF16),
            jax.ShapeDtypeStruct((b, ATTN_W, s), BF16),
            jax.ShapeDtypeStruct((b, ATTN_W, s), BF16),
            jax.ShapeDtypeStruct((b, s, ATTN_W), BF16),
        ],
        scratch_shapes=[pltpu.VMEM((SUBLANES, ATTN_W), F32),
                        pltpu.VMEM((rows, 3 * ATTN_W), F32)],
        compiler_params=pltpu.CompilerParams(
            dimension_semantics=("arbitrary", "arbitrary"), vmem_limit_bytes=VMEM_LIMIT["inproj"]),
        name="inproj",
    )(x, win, conv_w)


def _attn_kernel(qt_ref, k_ref, vt_ref, lq1_ref, lk1_ref, lq2_ref, lk2_ref, g_ref, w1_ref, w2_ref, w3_ref,
                 o_ref, w1b_ref, w2b_ref, w3b_ref, qpad_ref, m_ref, acc_ref, s_ref, bias_ref, *, t, q_per_step):
    for w_ref, wb_ref in ((w1_ref, w1b_ref), (w2_ref, w2b_ref), (w3_ref, w3b_ref)):
        wb_ref[...] = w_ref[...].astype(BF16)

    n_chains = 2 * N_HEADS
    hq = t // 2
    left, right = slice(0, hq), slice(hq, t)
    top_keys, bot_keys = slice(0, hq), slice(hq, t)

    @pl.when(jnp.logical_and(pl.program_id(0) == 0, pl.program_id(1) == 0))
    def _():
        kpos = lax.broadcasted_iota(jnp.int32, (hq, hq), 0)
        qpos = lax.broadcasted_iota(jnp.int32, (hq, hq), 1)
        bias_ref[...] = jnp.where(kpos <= qpos, 0.0, -jnp.inf).astype(F32)

    lam = (jnp.exp(jnp.sum(lq1_ref[...] * lk1_ref[...], keepdims=True))
           - jnp.exp(jnp.sum(lq2_ref[...] * lk2_ref[...], keepdims=True)) + LAMBDA_INIT)

    def query_tile(u):
        qi = pl.program_id(1) * q_per_step + u
        qcols = slice(u * t, (u + 1) * t)
        qpad, m, acc = qpad_ref.at[u], m_ref.at[u], acc_ref.at[u]
        top = lax.broadcasted_iota(jnp.int32, (V_DIM, t), 0) < QK_DIM
        for h in range(N_HEADS):
            qt = qt_ref[h * V_DIM:(h + 1) * V_DIM, qcols]
            zero = jnp.zeros_like(qt)
            qpad[2 * h] = jnp.where(top, qt, zero)
            qpad[2 * h + 1] = jnp.where(top, zero, qt)

        def softmax_pv(c, s_cols, keys, vt_cols, m_new):
            h = c // 2
            p = jnp.exp2(s_ref[c % S_SLOTS, keys, s_cols] - m_new).astype(BF16)
            vtblk = vt_ref[h * V_DIM:(h + 1) * V_DIM, vt_cols]
            ones_rows = jnp.ones((L_ROWS, vtblk.shape[1]), BF16)
            return jnp.dot(jnp.concatenate([vtblk, ones_rows], axis=0), p, preferred_element_type=F32)

        def scores(start, c):
            h = c // 2
            kblk = k_ref[pl.ds(start, t), h * V_DIM:(h + 1) * V_DIM]
            s_ref[c % S_SLOTS] = jnp.dot(kblk, qpad[c], preferred_element_type=F32)

        def diag_scores(start, c):
            h = c // 2
            ktop = k_ref[pl.ds(start, hq), h * V_DIM:(h + 1) * V_DIM]
            kbot = k_ref[pl.ds(pl.multiple_of(start + hq, hq), hq), h * V_DIM:(h + 1) * V_DIM]
            s_ref[c % S_SLOTS, top_keys, :] = jnp.dot(ktop, qpad[c], preferred_element_type=F32)
            s_ref[c % S_SLOTS, bot_keys, right] = jnp.dot(kbot, qpad[c, :, right], preferred_element_type=F32)

        def diag_tile(start):
            for c in range(n_chains):
                ahead = c + LOOKAHEAD
                if ahead < n_chains:
                    diag_scores(start, ahead)
                else:
                    scores(0, ahead - n_chains)
                slot = c % S_SLOTS
                s_ref[slot, top_keys, left] = s_ref[slot, top_keys, left] + bias_ref[...]
                s_ref[slot, bot_keys, right] = s_ref[slot, bot_keys, right] + bias_ref[...]
                m_top = jnp.max(s_ref[slot, top_keys, :], axis=0, keepdims=True)
                m_bot = jnp.max(s_ref[slot, bot_keys, right], axis=0, keepdims=True)
                m_new = jnp.concatenate([m_top[:, left], jnp.maximum(m_top[:, right], m_bot)], axis=1)
                m[c:c + 1, :] = m_new
                pv = softmax_pv(c, slice(None), top_keys, pl.ds(start, hq), m_new)
                pv_r = softmax_pv(c, right, bot_keys, pl.ds(pl.multiple_of(start + hq, hq), hq), m_new[:, right])
                acc[c, :, left] = pv[:, left]
                acc[c, :, right] = pv[:, right] + pv_r

        def full_tile(j, carry):
            start = pl.multiple_of(j * t, t)
            for c in range(n_chains):
                ahead = c + LOOKAHEAD
                if ahead < n_chains:
                    scores(start, ahead)
                else:
                    scores(pl.multiple_of(start + t, t), ahead - n_chains)
                m_old = m[c:c + 1, :]
                m_new = jnp.maximum(m_old, jnp.max(s_ref[c % S_SLOTS], axis=0, keepdims=True))
                m[c:c + 1, :] = m_new
                pv = softmax_pv(c, slice(None), slice(None), pl.ds(start, t), m_new)
                acc[c] = jnp.exp2(m_old - m_new) * acc[c] + pv
            return carry

        diag_start = pl.multiple_of(qi * t, t)
        for c in range(LOOKAHEAD):
            diag_scores(diag_start, c)
        diag_tile(diag_start)
        lax.fori_loop(0, qi, full_tile, 0)

        for h in range(N_HEADS):
            c1, c2 = 2 * h, 2 * h + 1
            o = (acc[c1, :V_DIM, :] / acc[c1, V_DIM:V_DIM + 1, :]
                 - lam * (acc[c2, :V_DIM, :] / acc[c2, V_DIM:V_DIM + 1, :]))
            o = o * lax.rsqrt(jnp.mean(o * o, axis=0, keepdims=True) + RMS_EPS)
            o_ref[qcols, h * V_DIM:(h + 1) * V_DIM] = (o.T * g_ref[...] * (1.0 - LAMBDA_INIT)).astype(BF16)

    for u in range(q_per_step):
        query_tile(u)


def _attention(qt, k, vt, lq1, lk1, lq2, lk2, g_row, weights, *, t, q_per_step):
    b, s, _ = k.shape
    n_chains = 2 * N_HEADS
    tq = q_per_step * t
    steps_per_batch = s // tq
    assert s % tq == 0 and n_chains % S_SLOTS == 0 and LOOKAHEAD < S_SLOTS
    assert all(w.shape[0] % (16 * b * steps_per_batch) == 0 for w in weights)
    lam_blk = pl.BlockSpec((1, QK_DIM), lambda bi, qi: (0, 0))
    slabs = [pl.BlockSpec((w.shape[0] // (b * steps_per_batch), w.shape[1]),
                          lambda bi, qi: (bi * steps_per_batch + qi, 0)) for w in weights]
    return pl.pallas_call(
        functools.partial(_attn_kernel, t=t, q_per_step=q_per_step),
        grid=(b, s // tq),
        in_specs=[
            pl.BlockSpec((None, ATTN_W, tq), lambda bi, qi: (bi, 0, qi)),
            pl.BlockSpec((None, s, ATTN_W), lambda bi, qi: (bi, 0, 0)),
            pl.BlockSpec((None, ATTN_W, s), lambda bi, qi: (bi, 0, 0)),
            lam_blk, lam_blk, lam_blk, lam_blk,
            pl.BlockSpec((1, V_DIM), lambda bi, qi: (0, 0)),
            *slabs,
        ],
        out_specs=[pl.BlockSpec((None, tq, ATTN_W), lambda bi, qi: (bi, qi, 0)), *slabs],
        out_shape=[jax.ShapeDtypeStruct((b, s, ATTN_W), BF16),
                   *[jax.ShapeDtypeStruct(w.shape, BF16) for w in weights]],
        scratch_shapes=[
            pltpu.VMEM((q_per_step, n_chains, V_DIM, t), BF16),
            pltpu.VMEM((q_per_step, n_chains, t), F32),
            pltpu.VMEM((q_per_step, n_chains, V_DIM + L_ROWS, t), F32),
            pltpu.VMEM((S_SLOTS, t, t), F32),
            pltpu.VMEM((t // 2, t // 2), F32),
        ],
        compiler_params=pltpu.CompilerParams(
            dimension_semantics=("arbitrary", "arbitrary"), vmem_limit_bytes=VMEM_LIMIT["diffattn"]),
        name="diffattn",
    )(qt, k, vt, lq1, lk1, lq2, lk2, g_row, *weights)


def _layer_norm(v, g, b):
    mu = jnp.mean(v, axis=-1, keepdims=True)
    c = v - mu
    var = jnp.mean(c * c, axis=-1, keepdims=True)
    return c * lax.rsqrt(var + LN_EPS) * g + b


def _tail_kernel(x_ref, attn_ref, conv_ref, wo_ref, ln1g_ref, ln1b_ref, wup_ref, fcw_ref, fcb_ref,
                 wdn_ref, ln2g_ref, ln2b_ref, o_ref, gate_ref, val_ref, ghist_ref, h1_ref, h1b_ref, act_ref,
                 *, rows, d_ff, chunks):
    @pl.when(pl.program_id(1) == 0)
    def _():
        ghist_ref[...] = jnp.zeros(ghist_ref.shape, F32)

    pieces = tuple(pl.ds(r, SUB_ROWS // 2) for r in range(0, rows, SUB_ROWS // 2))
    for rs in pieces:
        mix = (jnp.dot(attn_ref[rs, :], wo_ref[:ATTN_W, :], preferred_element_type=F32)
               + jnp.dot(conv_ref[rs, :], wo_ref[ATTN_W:, :], preferred_element_type=F32))
        h1 = _layer_norm(DEEPNORM_ALPHA * x_ref[rs, :] + mix, ln1g_ref[...], ln1b_ref[...])
        h1_ref[rs, :] = h1
        h1b_ref[rs, :] = h1.astype(BF16)

    for sub in range(rows // SUB_ROWS):
        sub_rows = pl.ds(sub * SUB_ROWS, SUB_ROWS)
        act = act_ref.at[sub % 2]
        for n, (c0, cw) in enumerate(chunks):
            cols = slice(c0, c0 + cw)
            gate = gate_ref.at[n % 2, :, 0:cw]
            val = val_ref.at[n % 2, :, 0:cw]
            gate[...] = jnp.dot(h1b_ref[sub_rows, :], wup_ref[:, cols], preferred_element_type=F32)
            val[...] = jnp.dot(h1b_ref[sub_rows, :], wup_ref[:, d_ff + c0:d_ff + c0 + cw],
                               preferred_element_type=F32)
            gc = _causal_conv3(gate[...], ghist_ref[:, cols], fcw_ref[:, cols]) + fcb_ref[:, cols]
            ghist_ref[:, cols] = gate[pl.ds(SUB_ROWS - SUBLANES, SUBLANES), :]
            act[:, cols] = (gc * jax.nn.sigmoid(gc) * val[...]).astype(BF16)
        for half in range(2):
            local = pl.ds(half * (SUB_ROWS // 2), SUB_ROWS // 2)
            rs = pieces[2 * sub + half]
            ffn = jnp.dot(act[local, :], wdn_ref[...], preferred_element_type=F32)
            o_ref[rs, :] = _layer_norm(DEEPNORM_ALPHA * h1_ref[rs, :] + ffn, ln2g_ref[...], ln2b_ref[...])


def _tail(x, attn, conv, wo, ln1g, ln1b, wup, fcw, fcb, wdn, ln2g, ln2b, *, rows, chunk):
    b, s, d = x.shape
    d_ff = wdn.shape[0]
    assert s % rows == 0 and rows % SUB_ROWS == 0 and wup.shape == (d, 2 * d_ff) and d_ff % 128 == 0
    chunks = tuple((c0, min(chunk, d_ff - c0)) for c0 in range(0, d_ff, chunk))
    half_blk = pl.BlockSpec((None, rows, ATTN_W), lambda bi, ti: (bi, ti, 0))
    full_blk = pl.BlockSpec((None, rows, d), lambda bi, ti: (bi, ti, 0))
    return pl.pallas_call(
        functools.partial(_tail_kernel, rows=rows, d_ff=d_ff, chunks=chunks),
        grid=(b, s // rows),
        in_specs=[full_blk, half_blk, half_blk,
                  _resident(wo.shape), _resident(ln1g.shape), _resident(ln1b.shape),
                  _resident(wup.shape), _resident(fcw.shape), _resident(fcb.shape), _resident(wdn.shape),
                  _resident(ln2g.shape), _resident(ln2b.shape)],
        out_specs=full_blk,
        out_shape=jax.ShapeDtypeStruct((b, s, d), x.dtype),
        scratch_shapes=[pltpu.VMEM((2, SUB_ROWS, chunk), F32),
                        pltpu.VMEM((2, SUB_ROWS, chunk), F32),
                        pltpu.VMEM((SUBLANES, d_ff), F32),
                        pltpu.VMEM((rows, d), F32),
                        pltpu.VMEM((rows, d), BF16),
                        pltpu.VMEM((2, SUB_ROWS, d_ff), BF16)],
        compiler_params=pltpu.CompilerParams(
            dimension_semantics=("arbitrary", "arbitrary"), vmem_limit_bytes=VMEM_LIMIT["tail"]),
        name="tail",
    )(x, attn, conv, wo, ln1g, ln1b, wup, fcw, fcb, wdn, ln2g, ln2b)


def kernel(x, w_in, lambda_q1, lambda_k1, lambda_q2, lambda_k2, attn_norm_g, conv_w, w_out, ln1_g, ln1_b,
           ffn_w_up, ffn_conv_w, ffn_conv_b, ffn_w_down, ln2_g, ln2_b):
    win = w_in[0].astype(BF16)

    k, qt, vt, conv = _inproj(x, win, conv_w, rows=INPROJ_ROWS)
    attn, wo, wup, wdn = _attention(qt, k, vt, lambda_q1, lambda_k1, lambda_q2, lambda_k2,
                                    attn_norm_g, (w_out[0], ffn_w_up[0], ffn_w_down[0]),
                                    t=ATTN_TILE, q_per_step=ATTN_Q_PER_STEP)
    return _tail(x, attn, conv, wo, ln1_g, ln1_b, wup, ffn_conv_w, ffn_conv_b, wdn, ln2_g, ln2_b,
                 rows=TAIL_ROWS, chunk=FFN_CHUNK)
```

```python
import functools
import math

import jax
import jax.numpy as jnp
from jax import lax
from jax.experimental import pallas as pl
from jax.experimental.pallas import tpu as pltpu

F32 = jnp.float32
BF16 = jnp.bfloat16

N_HEADS = 4
QK_DIM = 64
V_DIM = 2 * QK_DIM
ATTN_W = N_HEADS * V_DIM
LN_EPS = 1e-5
RMS_EPS = 1e-5
LAMBDA_INIT = 0.8 - 0.6 * math.exp(-0.3 * 0)
DEEPNORM_ALPHA = 2.0 ** 0.25
SCORE_SCALE = QK_DIM ** -0.5 * math.log2(math.e)

SUBLANES = 8
L_ROWS = 16
MIB = 1024 * 1024

INPROJ_ROWS = 1024
ATTN_TILE = 512
ATTN_Q_PER_STEP = 2
LOOKAHEAD = 2
S_SLOTS = 4
TAIL_ROWS = 1024
SUB_ROWS = 512
FFN_CHUNK = 256
VMEM_LIMIT = {"inproj": 40 * MIB, "diffattn": 48 * MIB, "tail": 56 * MIB}


def _resident(shape):
    block = (None, *shape[1:]) if len(shape) == 3 else shape
    return pl.BlockSpec(block, lambda *_: (0,) * len(shape), pipeline_mode=pl.Buffered(1))


def _causal_conv3(cur, hist, w):
    row = lax.broadcasted_iota(jnp.int32, hist.shape, 0)

    def shifted(s):
        rolled = pltpu.roll(cur, s, axis=0)
        head = jnp.where(row < s, pltpu.roll(hist, s, axis=0), rolled[:SUBLANES])
        return jnp.concatenate([head, rolled[SUBLANES:]], axis=0)

    return w[0:1, :] * shifted(2) + w[1:2, :] * shifted(1) + w[2:3, :] * cur


def _inproj_kernel(x_ref, win_ref, convw_ref, k_ref, qt_ref, vt_ref, conv_ref, uhist_ref, gates_ref,
                   *, rows):
    @pl.when(pl.program_id(1) == 0)
    def _():
        uhist_ref[...] = jnp.zeros(uhist_ref.shape, F32)

    xb = x_ref[...].astype(BF16)
    q = jnp.dot(xb, win_ref[:, :ATTN_W], preferred_element_type=F32) * SCORE_SCALE
    qt_ref[...] = q.astype(BF16).T
    v = jnp.dot(xb, win_ref[:, 2 * ATTN_W:3 * ATTN_W], preferred_element_type=F32)
    vt_ref[...] = v.astype(BF16).T

    half = ATTN_W // 2
    for c0 in (0, half):
        for part in range(3):
            src = 3 * ATTN_W + part * ATTN_W + c0
            gates_ref[:, part * ATTN_W + c0:part * ATTN_W + c0 + half] = jnp.dot(
                xb, win_ref[:, src:src + half], preferred_element_type=F32)
    k_ref[...] = jnp.dot(xb, win_ref[:, ATTN_W:2 * ATTN_W], preferred_element_type=F32).astype(BF16)
    for c0 in (0, half):
        cols = slice(c0, c0 + half)
        u = gates_ref[:, ATTN_W + c0:ATTN_W + c0 + half] * gates_ref[:, 2 * ATTN_W + c0:2 * ATTN_W + c0 + half]
        conv = _causal_conv3(u, uhist_ref[:, cols], convw_ref[:, cols])
        conv_ref[:, cols] = (gates_ref[:, cols] * conv).astype(BF16)
        uhist_ref[:, cols] = u[rows - SUBLANES:]


def _inproj(x, win, conv_w, *, rows):
    b, s, d = x.shape
    assert s % rows == 0 and win.shape == (d, 6 * ATTN_W) and conv_w.shape == (1, 3, ATTN_W)
    grid = (b, s // rows)
    row_blk = pl.BlockSpec((None, rows, ATTN_W), lambda bi, ti: (bi, ti, 0))
    col_blk = pl.BlockSpec((None, ATTN_W, rows), lambda bi, ti: (bi, 0, ti))
    return pl.pallas_call(
        functools.partial(_inproj_kernel, rows=rows),
        grid=grid,
        in_specs=[
            pl.BlockSpec((None, rows, d), lambda bi, ti: (bi, ti, 0)),
            _resident(win.shape),
            _resident(conv_w.shape),
        ],
        out_specs=[row_blk, col_blk, col_blk, row_blk],
        out_shape=[
            jax.ShapeDtypeStruct((b, s, ATTN_W), BF16),
            jax.ShapeDtypeStruct((b, ATTN_W, s), BF16),
            jax.ShapeDtypeStruct((b, ATTN_W, s), BF16),
            jax.ShapeDtypeStruct((b, s, ATTN_W), BF16),
        ],
        scratch_shapes=[pltpu.VMEM((SUBLANES, ATTN_W), F32),
                        pltpu.VMEM((rows, 3 * ATTN_W), F32)],
        compiler_params=pltpu.CompilerParams(
            dimension_semantics=("arbitrary", "arbitrary"), vmem_limit_bytes=VMEM_LIMIT["inproj"]),
        name="inproj",
    )(x, win, conv_w)


def _attn_kernel(qt_ref, k_ref, vt_ref, lq1_ref, lk1_ref, lq2_ref, lk2_ref, g_ref, w1_ref, w2_ref, w3_ref,
                 o_ref, w1b_ref, w2b_ref, w3b_ref, qpad_ref, m_ref, acc_ref, s_ref, bias_ref, *, t, q_per_step):
    for w_ref, wb_ref in ((w1_ref, w1b_ref), (w2_ref, w2b_ref), (w3_ref, w3b_ref)):
        wb_ref[...] = w_ref[...].astype(BF16)

    n_chains = 2 * N_HEADS
    hq = t // 2
    left, right = slice(0, hq), slice(hq, t)
    top_keys, bot_keys = slice(0, hq), slice(hq, t)

    @pl.when(jnp.logical_and(pl.program_id(0) == 0, pl.program_id(1) == 0))
    def _():
        kpos = lax.broadcasted_iota(jnp.int32, (hq, hq), 0)
        qpos = lax.broadcasted_iota(jnp.int32, (hq, hq), 1)
        bias_ref[...] = jnp.where(kpos <= qpos, 0.0, -jnp.inf).astype(F32)

    lam = (jnp.exp(jnp.sum(lq1_ref[...] * lk1_ref[...], keepdims=True))
           - jnp.exp(jnp.sum(lq2_ref[...] * lk2_ref[...], keepdims=True)) + LAMBDA_INIT)
    gain = g_ref[...] * (1.0 - LAMBDA_INIT)

    def query_tile(u):
        qi = pl.program_id(1) * q_per_step + u
        qcols = slice(u * t, (u + 1) * t)
        qpad, m, acc = qpad_ref.at[u], m_ref.at[u], acc_ref.at[u]
        top = lax.broadcasted_iota(jnp.int32, (V_DIM, t), 0) < QK_DIM
        for h in range(N_HEADS):
            qt = qt_ref[h * V_DIM:(h + 1) * V_DIM, qcols]
            zero = jnp.zeros_like(qt)
            qpad[2 * h] = jnp.where(top, qt, zero)
            qpad[2 * h + 1] = jnp.where(top, zero, qt)

        def softmax_pv(c, s_cols, keys, vt_cols, m_new):
            h = c // 2
            p = jnp.exp2(s_ref[c % S_SLOTS, keys, s_cols] - m_new).astype(BF16)
            vtblk = vt_ref[h * V_DIM:(h + 1) * V_DIM, vt_cols]
            ones_rows = jnp.ones((L_ROWS, vtblk.shape[1]), BF16)
            return jnp.dot(jnp.concatenate([vtblk, ones_rows], axis=0), p, preferred_element_type=F32)

        def scores(start, c):
            h = c // 2
            kblk = k_ref[pl.ds(start, t), h * V_DIM:(h + 1) * V_DIM]
            s_ref[c % S_SLOTS] = jnp.dot(kblk, qpad[c], preferred_element_type=F32)

        def diag_scores(start, c):
            h = c // 2
            ktop = k_ref[pl.ds(start, hq), h * V_DIM:(h + 1) * V_DIM]
            kbot = k_ref[pl.ds(pl.multiple_of(start + hq, hq), hq), h * V_DIM:(h + 1) * V_DIM]
            s_ref[c % S_SLOTS, top_keys, :] = jnp.dot(ktop, qpad[c], preferred_element_type=F32)
            s_ref[c % S_SLOTS, bot_keys, right] = jnp.dot(kbot, qpad[c, :, right], preferred_element_type=F32)

        def diag_tile(start):
            for c in range(n_chains):
                ahead = c + LOOKAHEAD
                if ahead < n_chains:
                    diag_scores(start, ahead)
                else:
                    scores(0, ahead - n_chains)
                slot = c % S_SLOTS
                s_ref[slot, top_keys, left] = s_ref[slot, top_keys, left] + bias_ref[...]
                s_ref[slot, bot_keys, right] = s_ref[slot, bot_keys, right] + bias_ref[...]
                m_top = jnp.max(s_ref[slot, top_keys, :], axis=0, keepdims=True)
                m_bot = jnp.max(s_ref[slot, bot_keys, right], axis=0, keepdims=True)
                m_new = jnp.concatenate([m_top[:, left], jnp.maximum(m_top[:, right], m_bot)], axis=1)
                m[c:c + 1, :] = m_new
                pv = softmax_pv(c, slice(None), top_keys, pl.ds(start, hq), m_new)
                pv_r = softmax_pv(c, right, bot_keys, pl.ds(pl.multiple_of(start + hq, hq), hq), m_new[:, right])
                acc[c, :, left] = pv[:, left]
                acc[c, :, right] = pv[:, right] + pv_r

        def full_tile(j, carry):
            start = pl.multiple_of(j * t, t)
            for c in range(n_chains):
                ahead = c + LOOKAHEAD
                if ahead < n_chains:
                    scores(start, ahead)
                else:
                    scores(pl.multiple_of(start + t, t), ahead - n_chains)
                m_old = m[c:c + 1, :]
                m_new = jnp.maximum(m_old, jnp.max(s_ref[c % S_SLOTS], axis=0, keepdims=True))
                m[c:c + 1, :] = m_new
                pv = softmax_pv(c, slice(None), slice(None), pl.ds(start, t), m_new)
                acc[c] = jnp.exp2(m_old - m_new) * acc[c] + pv
            return carry

        diag_start = pl.multiple_of(qi * t, t)
        for c in range(LOOKAHEAD):
            diag_scores(diag_start, c)
        diag_tile(diag_start)
        lax.fori_loop(0, qi, full_tile, 0)

        for h in range(N_HEADS):
            c1, c2 = 2 * h, 2 * h + 1
            o = (acc[c1, :V_DIM, :] * (1.0 / acc[c1, V_DIM:V_DIM + 1, :])
                 - acc[c2, :V_DIM, :] * (lam / acc[c2, V_DIM:V_DIM + 1, :]))
            o = o * lax.rsqrt(jnp.mean(o * o, axis=0, keepdims=True) + RMS_EPS)
            o_ref[qcols, h * V_DIM:(h + 1) * V_DIM] = (o.T * gain).astype(BF16)

    for u in range(q_per_step):
        query_tile(u)


def _attention(qt, k, vt, lq1, lk1, lq2, lk2, g_row, weights, *, t, q_per_step):
    b, s, _ = k.shape
    n_chains = 2 * N_HEADS
    tq = q_per_step * t
    steps_per_batch = s // tq
    assert s % tq == 0 and n_chains % S_SLOTS == 0 and LOOKAHEAD < S_SLOTS
    assert all(w.shape[0] % (16 * b * steps_per_batch) == 0 for w in weights)
    lam_blk = pl.BlockSpec((1, QK_DIM), lambda bi, qi: (0, 0))
    slabs = [pl.BlockSpec((w.shape[0] // (b * steps_per_batch), w.shape[1]),
                          lambda bi, qi: (bi * steps_per_batch + qi, 0)) for w in weights]
    return pl.pallas_call(
        functools.partial(_attn_kernel, t=t, q_per_step=q_per_step),
        grid=(b, s // tq),
        in_specs=[
            pl.BlockSpec((None, ATTN_W, tq), lambda bi, qi: (bi, 0, qi)),
            pl.BlockSpec((None, s, ATTN_W), lambda bi, qi: (bi, 0, 0)),
            pl.BlockSpec((None, ATTN_W, s), lambda bi, qi: (bi, 0, 0)),
            lam_blk, lam_blk, lam_blk, lam_blk,
            pl.BlockSpec((1, V_DIM), lambda bi, qi: (0, 0)),
            *slabs,
        ],
        out_specs=[pl.BlockSpec((None, tq, ATTN_W), lambda bi, qi: (bi, qi, 0)), *slabs],
        out_shape=[jax.ShapeDtypeStruct((b, s, ATTN_W), BF16),
                   *[jax.ShapeDtypeStruct(w.shape, BF16) for w in weights]],
        scratch_shapes=[
            pltpu.VMEM((q_per_step, n_chains, V_DIM, t), BF16),
            pltpu.VMEM((q_per_step, n_chains, t), F32),
            pltpu.VMEM((q_per_step, n_chains, V_DIM + L_ROWS, t), F32),
            pltpu.VMEM((S_SLOTS, t, t), F32),
            pltpu.VMEM((t // 2, t // 2), F32),
        ],
        compiler_params=pltpu.CompilerParams(
            dimension_semantics=("arbitrary", "arbitrary"), vmem_limit_bytes=VMEM_LIMIT["diffattn"]),
        name="diffattn",
    )(qt, k, vt, lq1, lk1, lq2, lk2, g_row, *weights)


def _layer_norm(v, g, b):
    mu = jnp.mean(v, axis=-1, keepdims=True)
    c = v - mu
    var = jnp.mean(c * c, axis=-1, keepdims=True)
    return c * lax.rsqrt(var + LN_EPS) * g + b


def _tail_kernel(x_ref, attn_ref, conv_ref, wo_ref, ln1g_ref, ln1b_ref, wup_ref, fcw_ref, fcb_ref,
                 wdn_ref, ln2g_ref, ln2b_ref, o_ref, gate_ref, val_ref, ghist_ref, h1_ref, h1b_ref, act_ref,
                 *, rows, d_ff, chunks):
    @pl.when(pl.program_id(1) == 0)
    def _():
        ghist_ref[...] = jnp.zeros(ghist_ref.shape, F32)

    pieces = tuple(pl.ds(r, SUB_ROWS // 2) for r in range(0, rows, SUB_ROWS // 2))
    for rs in pieces:
        mix = (jnp.dot(attn_ref[rs, :], wo_ref[:ATTN_W, :], preferred_element_type=F32)
               + jnp.dot(conv_ref[rs, :], wo_ref[ATTN_W:, :], preferred_element_type=F32))
        h1 = _layer_norm(DEEPNORM_ALPHA * x_ref[rs, :] + mix, ln1g_ref[...], ln1b_ref[...])
        h1_ref[rs, :] = h1
        h1b_ref[rs, :] = h1.astype(BF16)

    for sub in range(rows // SUB_ROWS):
        sub_rows = pl.ds(sub * SUB_ROWS, SUB_ROWS)
        act = act_ref.at[sub % 2]
        for n, (c0, cw) in enumerate(chunks):
            cols = slice(c0, c0 + cw)
            gate = gate_ref.at[n % 2, :, 0:cw]
            val = val_ref.at[n % 2, :, 0:cw]
            gate[...] = jnp.dot(h1b_ref[sub_rows, :], wup_ref[:, cols], preferred_element_type=F32)
            val[...] = jnp.dot(h1b_ref[sub_rows, :], wup_ref[:, d_ff + c0:d_ff + c0 + cw],
                               preferred_element_type=F32)
            gc = _causal_conv3(gate[...], ghist_ref[:, cols], fcw_ref[:, cols]) + fcb_ref[:, cols]
            ghist_ref[:, cols] = gate[pl.ds(SUB_ROWS - SUBLANES, SUBLANES), :]
            act[:, cols] = (gc * jax.nn.sigmoid(gc) * val[...]).astype(BF16)
        for half in range(2):
            local = pl.ds(half * (SUB_ROWS // 2), SUB_ROWS // 2)
            rs = pieces[2 * sub + half]
            ffn = jnp.dot(act[local, :], wdn_ref[...], preferred_element_type=F32)
            o_ref[rs, :] = _layer_norm(DEEPNORM_ALPHA * h1_ref[rs, :] + ffn, ln2g_ref[...], ln2b_ref[...])


def _tail(x, attn, conv, wo, ln1g, ln1b, wup, fcw, fcb, wdn, ln2g, ln2b, *, rows, chunk):
    b, s, d = x.shape
    d_ff = wdn.shape[0]
    assert s % rows == 0 and rows % SUB_ROWS == 0 and wup.shape == (d, 2 * d_ff) and d_ff % 128 == 0
    chunks = tuple((c0, min(chunk, d_ff - c0)) for c0 in range(0, d_ff, chunk))
    half_blk = pl.BlockSpec((None, rows, ATTN_W), lambda bi, ti: (bi, ti, 0))
    full_blk = pl.BlockSpec((None, rows, d), lambda bi, ti: (bi, ti, 0))
    return pl.pallas_call(
        functools.partial(_tail_kernel, rows=rows, d_ff=d_ff, chunks=chunks),
        grid=(b, s // rows),
        in_specs=[full_blk, half_blk, half_blk,
                  _resident(wo.shape), _resident(ln1g.shape), _resident(ln1b.shape),
                  _resident(wup.shape), _resident(fcw.shape), _resident(fcb.shape), _resident(wdn.shape),
                  _resident(ln2g.shape), _resident(ln2b.shape)],
        out_specs=full_blk,
        out_shape=jax.ShapeDtypeStruct((b, s, d), x.dtype),
        scratch_shapes=[pltpu.VMEM((2, SUB_ROWS, chunk), F32),
                        pltpu.VMEM((2, SUB_ROWS, chunk), F32),
                        pltpu.VMEM((SUBLANES, d_ff), F32),
                        pltpu.VMEM((rows, d), F32),
                        pltpu.VMEM((rows, d), BF16),
                        pltpu.VMEM((2, SUB_ROWS, d_ff), BF16)],
        compiler_params=pltpu.CompilerParams(
            dimension_semantics=("arbitrary", "arbitrary"), vmem_limit_bytes=VMEM_LIMIT["tail"]),
        name="tail",
    )(x, attn, conv, wo, ln1g, ln1b, wup, fcw, fcb, wdn, ln2g, ln2b)


def kernel(x, w_in, lambda_q1, lambda_k1, lambda_q2, lambda_k2, attn_norm_g, conv_w, w_out, ln1_g, ln1_b,
           ffn_w_up, ffn_conv_w, ffn_conv_b, ffn_w_down, ln2_g, ln2_b):
    win = w_in[0].astype(BF16)

    k, qt, vt, conv = _inproj(x, win, conv_w, rows=INPROJ_ROWS)
    attn, wo, wup, wdn = _attention(qt, k, vt, lambda_q1, lambda_k1, lambda_q2, lambda_k2,
                                    attn_norm_g, (w_out[0], ffn_w_up[0], ffn_w_down[0]),
                                    t=ATTN_TILE, q_per_step=ATTN_Q_PER_STEP)
    return _tail(x, attn, conv, wo, ln1_g, ln1_b, wup, ffn_conv_w, ffn_conv_b, wdn, ln2_g, ln2_b,
                 rows=TAIL_ROWS, chunk=FFN_CHUNK)
```

```python
import functools
import math

import jax
import jax.numpy as jnp
from jax import lax
from jax.experimental import pallas as pl
from jax.experimental.pallas import tpu as pltpu

F32 = jnp.float32
BF16 = jnp.bfloat16

N_HEADS = 4
QK_DIM = 64
V_DIM = 2 * QK_DIM
ATTN_W = N_HEADS * V_DIM
LN_EPS = 1e-5
RMS_EPS = 1e-5
LAMBDA_INIT = 0.8 - 0.6 * math.exp(-0.3 * 0)
DEEPNORM_ALPHA = 2.0 ** 0.25
SCORE_SCALE = QK_DIM ** -0.5 * math.log2(math.e)

SUBLANES = 8
L_ROWS = 16
MIB = 1024 * 1024

INPROJ_ROWS = 1024
ATTN_TILE = 512
ATTN_Q_PER_STEP = 2
LOOKAHEAD = 2
S_SLOTS = 4
TAIL_ROWS = 1024
SUB_ROWS = 512
FFN_CHUNK = 256
VMEM_LIMIT = {"inproj": 52 * MIB, "diffattn": 48 * MIB, "tail": 56 * MIB}


def _resident(shape):
    block = (None, *shape[1:]) if len(shape) == 3 else shape
    return pl.BlockSpec(block, lambda *_: (0,) * len(shape), pipeline_mode=pl.Buffered(1))


def _causal_conv3(cur, hist, w):
    row = lax.broadcasted_iota(jnp.int32, hist.shape, 0)

    def shifted(s):
        rolled = pltpu.roll(cur, s, axis=0)
        head = jnp.where(row < s, pltpu.roll(hist, s, axis=0), rolled[:SUBLANES])
        return jnp.concatenate([head, rolled[SUBLANES:]], axis=0)

    return w[0:1, :] * shifted(2) + w[1:2, :] * shifted(1) + w[2:3, :] * cur


def _inproj_kernel(x_ref, winf_ref, convw_ref, k_ref, qt_ref, vt_ref, conv_ref, uhist_ref, gates_ref, win_ref,
                   *, rows):
    @pl.when(jnp.logical_and(pl.program_id(0) == 0, pl.program_id(1) == 0))
    def _():
        win_ref[...] = winf_ref[...].astype(BF16)

    @pl.when(pl.program_id(1) == 0)
    def _():
        uhist_ref[...] = jnp.zeros(uhist_ref.shape, F32)

    xb = x_ref[...].astype(BF16)
    q = jnp.dot(xb, win_ref[:, :ATTN_W], preferred_element_type=F32) * SCORE_SCALE
    qt_ref[...] = q.astype(BF16).T
    v = jnp.dot(xb, win_ref[:, 2 * ATTN_W:3 * ATTN_W], preferred_element_type=F32)
    vt_ref[...] = v.astype(BF16).T

    half = ATTN_W // 2
    for c0 in (0, half):
        for part in range(3):
            src = 3 * ATTN_W + part * ATTN_W + c0
            gates_ref[:, part * ATTN_W + c0:part * ATTN_W + c0 + half] = jnp.dot(
                xb, win_ref[:, src:src + half], preferred_element_type=F32)
    k_ref[...] = jnp.dot(xb, win_ref[:, ATTN_W:2 * ATTN_W], preferred_element_type=F32).astype(BF16)
    for c0 in (0, half):
        cols = slice(c0, c0 + half)
        u = gates_ref[:, ATTN_W + c0:ATTN_W + c0 + half] * gates_ref[:, 2 * ATTN_W + c0:2 * ATTN_W + c0 + half]
        conv = _causal_conv3(u, uhist_ref[:, cols], convw_ref[:, cols])
        conv_ref[:, cols] = (gates_ref[:, cols] * conv).astype(BF16)
        uhist_ref[:, cols] = u[rows - SUBLANES:]


def _inproj(x, win, conv_w, *, rows):
    b, s, d = x.shape
    assert s % rows == 0 and win.shape == (1, d, 6 * ATTN_W) and conv_w.shape == (1, 3, ATTN_W)
    grid = (b, s // rows)
    row_blk = pl.BlockSpec((None, rows, ATTN_W), lambda bi, ti: (bi, ti, 0))
    col_blk = pl.BlockSpec((None, ATTN_W, rows), lambda bi, ti: (bi, 0, ti))
    return pl.pallas_call(
        functools.partial(_inproj_kernel, rows=rows),
        grid=grid,
        in_specs=[
            pl.BlockSpec((None, rows, d), lambda bi, ti: (bi, ti, 0)),
            _resident(win.shape),
            _resident(conv_w.shape),
        ],
        out_specs=[row_blk, col_blk, col_blk, row_blk],
        out_shape=[
            jax.ShapeDtypeStruct((b, s, ATTN_W), BF16),
            jax.ShapeDtypeStruct((b, ATTN_W, s), BF16),
            jax.ShapeDtypeStruct((b, ATTN_W, s), BF16),
            jax.ShapeDtypeStruct((b, s, ATTN_W), BF16),
        ],
        scratch_shapes=[pltpu.VMEM((SUBLANES, ATTN_W), F32),
                        pltpu.VMEM((rows, 3 * ATTN_W), F32),
                        pltpu.VMEM((d, 6 * ATTN_W), BF16)],
        compiler_params=pltpu.CompilerParams(
            dimension_semantics=("arbitrary", "arbitrary"), vmem_limit_bytes=VMEM_LIMIT["inproj"]),
        name="inproj",
    )(x, win, conv_w)


def _attn_kernel(qt_ref, k_ref, vt_ref, lq1_ref, lk1_ref, lq2_ref, lk2_ref, g_ref, w1_ref, w2_ref, w3_ref,
                 o_ref, w1b_ref, w2b_ref, w3b_ref, qpad_ref, m_ref, acc_ref, s_ref, bias_ref, *, t, q_per_step):
    for w_ref, wb_ref in ((w1_ref, w1b_ref), (w2_ref, w2b_ref), (w3_ref, w3b_ref)):
        wb_ref[...] = w_ref[...].astype(BF16)

    n_chains = 2 * N_HEADS
    hq = t // 2
    left, right = slice(0, hq), slice(hq, t)
    top_keys, bot_keys = slice(0, hq), slice(hq, t)

    @pl.when(jnp.logical_and(pl.program_id(0) == 0, pl.program_id(1) == 0))
    def _():
        kpos = lax.broadcasted_iota(jnp.int32, (hq, hq), 0)
        qpos = lax.broadcasted_iota(jnp.int32, (hq, hq), 1)
        bias_ref[...] = jnp.where(kpos <= qpos, 0.0, -jnp.inf).astype(F32)

    lam = (jnp.exp(jnp.sum(lq1_ref[...] * lk1_ref[...], keepdims=True))
           - jnp.exp(jnp.sum(lq2_ref[...] * lk2_ref[...], keepdims=True)) + LAMBDA_INIT)
    gain = g_ref[...] * (1.0 - LAMBDA_INIT)

    def query_tile(u):
        qi = pl.program_id(1) * q_per_step + u
        qcols = slice(u * t, (u + 1) * t)
        qpad, m, acc = qpad_ref.at[u], m_ref.at[u], acc_ref.at[u]
        top = lax.broadcasted_iota(jnp.int32, (V_DIM, t), 0) < QK_DIM
        for h in range(N_HEADS):
            qt = qt_ref[h * V_DIM:(h + 1) * V_DIM, qcols]
            zero = jnp.zeros_like(qt)
            qpad[2 * h] = jnp.where(top, qt, zero)
            qpad[2 * h + 1] = jnp.where(top, zero, qt)

        def softmax_pv(c, s_cols, keys, vt_cols, m_new):
            h = c // 2
            p = jnp.exp2(s_ref[c % S_SLOTS, keys, s_cols] - m_new).astype(BF16)
            vtblk = vt_ref[h * V_DIM:(h + 1) * V_DIM, vt_cols]
            ones_rows = jnp.ones((L_ROWS, vtblk.shape[1]), BF16)
            return jnp.dot(jnp.concatenate([vtblk, ones_rows], axis=0), p, preferred_element_type=F32)

        def scores(start, c):
            h = c // 2
            kblk = k_ref[pl.ds(start, t), h * V_DIM:(h + 1) * V_DIM]
            s_ref[c % S_SLOTS] = jnp.dot(kblk, qpad[c], preferred_element_type=F32)

        def diag_scores(start, c):
            h = c // 2
            ktop = k_ref[pl.ds(start, hq), h * V_DIM:(h + 1) * V_DIM]
            kbot = k_ref[pl.ds(pl.multiple_of(start + hq, hq), hq), h * V_DIM:(h + 1) * V_DIM]
            s_ref[c % S_SLOTS, top_keys, :] = jnp.dot(ktop, qpad[c], preferred_element_type=F32)
            s_ref[c % S_SLOTS, bot_keys, right] = jnp.dot(kbot, qpad[c, :, right], preferred_element_type=F32)

        def diag_tile(start):
            for c in range(n_chains):
                ahead = c + LOOKAHEAD
                if ahead < n_chains:
                    diag_scores(start, ahead)
                else:
                    scores(0, ahead - n_chains)
                slot = c % S_SLOTS
                s_ref[slot, top_keys, left] = s_ref[slot, top_keys, left] + bias_ref[...]
                s_ref[slot, bot_keys, right] = s_ref[slot, bot_keys, right] + bias_ref[...]
                m_top = jnp.max(s_ref[slot, top_keys, :], axis=0, keepdims=True)
                m_bot = jnp.max(s_ref[slot, bot_keys, right], axis=0, keepdims=True)
                m_new = jnp.concatenate([m_top[:, left], jnp.maximum(m_top[:, right], m_bot)], axis=1)
                m[c:c + 1, :] = m_new
                pv = softmax_pv(c, slice(None), top_keys, pl.ds(start, hq), m_new)
                pv_r = softmax_pv(c, right, bot_keys, pl.ds(pl.multiple_of(start + hq, hq), hq), m_new[:, right])
                acc[c, :, left] = pv[:, left]
                acc[c, :, right] = pv[:, right] + pv_r

        def full_tile(j, carry):
            start = pl.multiple_of(j * t, t)
            for c in range(n_chains):
                ahead = c + LOOKAHEAD
                if ahead < n_chains:
                    scores(start, ahead)
                else:
                    scores(pl.multiple_of(start + t, t), ahead - n_chains)
                m_old = m[c:c + 1, :]
                m_new = jnp.maximum(m_old, jnp.max(s_ref[c % S_SLOTS], axis=0, keepdims=True))
                m[c:c + 1, :] = m_new
                pv = softmax_pv(c, slice(None), slice(None), pl.ds(start, t), m_new)
                acc[c] = jnp.exp2(m_old - m_new) * acc[c] + pv
            return carry

        diag_start = pl.multiple_of(qi * t, t)
        for c in range(LOOKAHEAD):
            diag_scores(diag_start, c)
        diag_tile(diag_start)
        lax.fori_loop(0, qi, full_tile, 0)

        for h in range(N_HEADS):
            c1, c2 = 2 * h, 2 * h + 1
            o = (acc[c1, :V_DIM, :] * (1.0 / acc[c1, V_DIM:V_DIM + 1, :])
                 - acc[c2, :V_DIM, :] * (lam / acc[c2, V_DIM:V_DIM + 1, :]))
            o = o * lax.rsqrt(jnp.mean(o * o, axis=0, keepdims=True) + RMS_EPS)
            o_ref[qcols, h * V_DIM:(h + 1) * V_DIM] = (o.T * gain).astype(BF16)

    for u in range(q_per_step):
        query_tile(u)


def _attention(qt, k, vt, lq1, lk1, lq2, lk2, g_row, weights, *, t, q_per_step):
    b, s, _ = k.shape
    n_chains = 2 * N_HEADS
    tq = q_per_step * t
    steps_per_batch = s // tq
    assert s % tq == 0 and n_chains % S_SLOTS == 0 and LOOKAHEAD < S_SLOTS
    assert all(w.shape[0] % (16 * b * steps_per_batch) == 0 for w in weights)
    lam_blk = pl.BlockSpec((1, QK_DIM), lambda bi, qi: (0, 0))
    slabs = [pl.BlockSpec((w.shape[0] // (b * steps_per_batch), w.shape[1]),
                          lambda bi, qi: (bi * steps_per_batch + qi, 0)) for w in weights]
    return pl.pallas_call(
        functools.partial(_attn_kernel, t=t, q_per_step=q_per_step),
        grid=(b, s // tq),
        in_specs=[
            pl.BlockSpec((None, ATTN_W, tq), lambda bi, qi: (bi, 0, qi)),
            pl.BlockSpec((None, s, ATTN_W), lambda bi, qi: (bi, 0, 0)),
            pl.BlockSpec((None, ATTN_W, s), lambda bi, qi: (bi, 0, 0)),
            lam_blk, lam_blk, lam_blk, lam_blk,
            pl.BlockSpec((1, V_DIM), lambda bi, qi: (0, 0)),
            *slabs,
        ],
        out_specs=[pl.BlockSpec((None, tq, ATTN_W), lambda bi, qi: (bi, qi, 0)), *slabs],
        out_shape=[jax.ShapeDtypeStruct((b, s, ATTN_W), BF16),
                   *[jax.ShapeDtypeStruct(w.shape, BF16) for w in weights]],
        scratch_shapes=[
            pltpu.VMEM((q_per_step, n_chains, V_DIM, t), BF16),
            pltpu.VMEM((q_per_step, n_chains, t), F32),
            pltpu.VMEM((q_per_step, n_chains, V_DIM + L_ROWS, t), F32),
            pltpu.VMEM((S_SLOTS, t, t), F32),
            pltpu.VMEM((t // 2, t // 2), F32),
        ],
        compiler_params=pltpu.CompilerParams(
            dimension_semantics=("arbitrary", "arbitrary"), vmem_limit_bytes=VMEM_LIMIT["diffattn"]),
        name="diffattn",
    )(qt, k, vt, lq1, lk1, lq2, lk2, g_row, *weights)


def _layer_norm(v, g, b):
    mu = jnp.mean(v, axis=-1, keepdims=True)
    c = v - mu
    var = jnp.mean(c * c, axis=-1, keepdims=True)
    return c * lax.rsqrt(var + LN_EPS) * g + b


def _tail_kernel(x_ref, attn_ref, conv_ref, wo_ref, ln1g_ref, ln1b_ref, wup_ref, fcw_ref, fcb_ref,
                 wdn_ref, ln2g_ref, ln2b_ref, o_ref, gate_ref, val_ref, ghist_ref, h1_ref, h1b_ref, act_ref,
                 *, rows, d_ff, chunks):
    @pl.when(pl.program_id(1) == 0)
    def _():
        ghist_ref[...] = jnp.zeros(ghist_ref.shape, F32)

    pieces = tuple(pl.ds(r, SUB_ROWS // 2) for r in range(0, rows, SUB_ROWS // 2))
    for rs in pieces:
        mix = (jnp.dot(attn_ref[rs, :], wo_ref[:ATTN_W, :], preferred_element_type=F32)
               + jnp.dot(conv_ref[rs, :], wo_ref[ATTN_W:, :], preferred_element_type=F32))
        h1 = _layer_norm(DEEPNORM_ALPHA * x_ref[rs, :] + mix, ln1g_ref[...], ln1b_ref[...])
        h1_ref[rs, :] = h1
        h1b_ref[rs, :] = h1.astype(BF16)

    for sub in range(rows // SUB_ROWS):
        sub_rows = pl.ds(sub * SUB_ROWS, SUB_ROWS)
        act = act_ref.at[sub % 2]
        for n, (c0, cw) in enumerate(chunks):
            cols = slice(c0, c0 + cw)
            gate = gate_ref.at[n % 2, :, 0:cw]
            val = val_ref.at[n % 2, :, 0:cw]
            gate[...] = jnp.dot(h1b_ref[sub_rows, :], wup_ref[:, cols], preferred_element_type=F32)
            val[...] = jnp.dot(h1b_ref[sub_rows, :], wup_ref[:, d_ff + c0:d_ff + c0 + cw],
                               preferred_element_type=F32)
            gc = _causal_conv3(gate[...], ghist_ref[:, cols], fcw_ref[:, cols]) + fcb_ref[:, cols]
            ghist_ref[:, cols] = gate[pl.ds(SUB_ROWS - SUBLANES, SUBLANES), :]
            act[:, cols] = (gc * jax.nn.sigmoid(gc) * val[...]).astype(BF16)
        for half in range(2):
            local = pl.ds(half * (SUB_ROWS // 2), SUB_ROWS // 2)
            rs = pieces[2 * sub + half]
            ffn = jnp.dot(act[local, :], wdn_ref[...], preferred_element_type=F32)
            o_ref[rs, :] = _layer_norm(DEEPNORM_ALPHA * h1_ref[rs, :] + ffn, ln2g_ref[...], ln2b_ref[...])


def _tail(x, attn, conv, wo, ln1g, ln1b, wup, fcw, fcb, wdn, ln2g, ln2b, *, rows, chunk):
    b, s, d = x.shape
    d_ff = wdn.shape[0]
    assert s % rows == 0 and rows % SUB_ROWS == 0 and wup.shape == (d, 2 * d_ff) and d_ff % 128 == 0
    chunks = tuple((c0, min(chunk, d_ff - c0)) for c0 in range(0, d_ff, chunk))
    half_blk = pl.BlockSpec((None, rows, ATTN_W), lambda bi, ti: (bi, ti, 0))
    full_blk = pl.BlockSpec((None, rows, d), lambda bi, ti: (bi, ti, 0))
    return pl.pallas_call(
        functools.partial(_tail_kernel, rows=rows, d_ff=d_ff, chunks=chunks),
        grid=(b, s // rows),
        in_specs=[full_blk, half_blk, half_blk,
                  _resident(wo.shape), _resident(ln1g.shape), _resident(ln1b.shape),
                  _resident(wup.shape), _resident(fcw.shape), _resident(fcb.shape), _resident(wdn.shape),
                  _resident(ln2g.shape), _resident(ln2b.shape)],
        out_specs=full_blk,
        out_shape=jax.ShapeDtypeStruct((b, s, d), x.dtype),
        scratch_shapes=[pltpu.VMEM((2, SUB_ROWS, chunk), F32),
                        pltpu.VMEM((2, SUB_ROWS, chunk), F32),
                        pltpu.VMEM((SUBLANES, d_ff), F32),
                        pltpu.VMEM((rows, d), F32),
                        pltpu.VMEM((rows, d), BF16),
                        pltpu.VMEM((2, SUB_ROWS, d_ff), BF16)],
        compiler_params=pltpu.CompilerParams(
            dimension_semantics=("arbitrary", "arbitrary"), vmem_limit_bytes=VMEM_LIMIT["tail"]),
        name="tail",
    )(x, attn, conv, wo, ln1g, ln1b, wup, fcw, fcb, wdn, ln2g, ln2b)


def kernel(x, w_in, lambda_q1, lambda_k1, lambda_q2, lambda_k2, attn_norm_g, conv_w, w_out, ln1_g, ln1_b,
           ffn_w_up, ffn_conv_w, ffn_conv_b, ffn_w_down, ln2_g, ln2_b):
    k, qt, vt, conv = _inproj(x, w_in, conv_w, rows=INPROJ_ROWS)
    attn, wo, wup, wdn = _attention(qt, k, vt, lambda_q1, lambda_k1, lambda_q2, lambda_k2,
                                    attn_norm_g, (w_out[0], ffn_w_up[0], ffn_w_down[0]),
                                    t=ATTN_TILE, q_per_step=ATTN_Q_PER_STEP)
    return _tail(x, attn, conv, wo, ln1_g, ln1_b, wup, ffn_conv_w, ffn_conv_b, wdn, ln2_g, ln2_b,
                 rows=TAIL_ROWS, chunk=FFN_CHUNK)
```
